```python
import math
import jax, jax.numpy as jnp
from jax import lax
import numpy as np

D_MODEL = 1024
BATCH = 32
SEQ = 2048
DEPTH = 2

HEAD_DIM = 64
N_HEADS = D_MODEL // HEAD_DIM
MIX_WIDTH = N_HEADS * HEAD_DIM
H_FOX = 5
H_MOBA = 5
H_DIL = N_HEADS - H_FOX - H_MOBA
ROT_DIM = HEAD_DIM // 4
ROPE_THETA = 500000.0
Q_BLOCK = 128
FORGET_BIAS_INIT = 3.0
MOBA_BLOCK = 256
MOBA_TOPK = 3
MOBA_Q_CHUNK = 16
DIL_PATTERNS = ((128, 1), (512, 4), (2048, 16))
BAND_BLOCK = 128
D_FF = 3584
N_EXPERTS = 8
TOP_K = 2
MOE_ROW_BLOCK = 256
EPS = 1e-6

kernel_name = "hymba_fox_moba_dilated_moe"


def rmsnorm(x, g):
    xf = x.astype(jnp.float32)
    y = xf * lax.rsqrt(jnp.mean(xf * xf, axis=-1, keepdims=True) + EPS)
    return (y * g.astype(jnp.float32)).astype(x.dtype)


def rope_tables(seq):
    inv = ROPE_THETA ** (-jnp.arange(0, ROT_DIM, 2, dtype=jnp.float32) / ROT_DIM)
    ang = jnp.arange(seq, dtype=jnp.float32)[:, None] * inv[None, :]
    return jnp.cos(ang), jnp.sin(ang)


def partial_rope(x, cos, sin):
    half = ROT_DIM // 2
    c = cos.astype(x.dtype)
    s = sin.astype(x.dtype)
    x1 = x[..., :half]
    x2 = x[..., half:ROT_DIM]
    return jnp.concatenate([x1 * c - x2 * s, x2 * c + x1 * s, x[..., ROT_DIM:]], axis=-1)


def forgetting_attention(q, k, v, log_f):
    B, H, S, hd = q.shape
    c = jnp.cumsum(log_f, axis=-1)
    scale = hd ** -0.5
    outs = []
    for blk in range(S // Q_BLOCK):
        q0, q1 = blk * Q_BLOCK, (blk + 1) * Q_BLOCK
        logits = jnp.einsum('bhqd,bhkd->bhqk', q[:, :, q0:q1], k[:, :, :q1],
                            preferred_element_type=jnp.float32) * scale
        logits = logits + c[:, :, q0:q1, None] - c[:, :, None, :q1]
        causal = jnp.arange(q0, q1)[:, None] >= jnp.arange(q1)[None, :]
        p = jax.nn.softmax(jnp.where(causal, logits, -jnp.inf), axis=-1)
        outs.append(jnp.einsum('bhqk,bhkd->bhqd', p.astype(v.dtype), v[:, :, :q1]))
    return jnp.concatenate(outs, axis=2)


def moba_attention(q, k, v):
    B, H, S, hd = q.shape
    n_blk = -(-S // MOBA_BLOCK)
    Sp = n_blk * MOBA_BLOCK
    pad = ((0, 0), (0, 0), (0, Sp - S), (0, 0))
    qp, kp, vp = jnp.pad(q, pad), jnp.pad(k, pad), jnp.pad(v, pad)
    k_blocks = kp.reshape(B, H, n_blk, MOBA_BLOCK, hd)
    v_blocks = vp.reshape(B, H, n_blk, MOBA_BLOCK, hd)
    k_mean = jnp.mean(k_blocks.astype(jnp.float32), axis=3).astype(k.dtype)
    topk = min(MOBA_TOPK, n_blk)
    scale = hd ** -0.5
    b_idx = jnp.arange(B)[:, None, None, None]
    h_idx = jnp.arange(H)[None, :, None, None]

    def chunk(ci):
        q0 = ci * MOBA_Q_CHUNK
        qc = lax.dynamic_slice_in_dim(qp, q0, MOBA_Q_CHUNK, axis=2)
        own = q0 // MOBA_BLOCK
        gate = jnp.einsum('bhqd,bhnd->bhqn', qc, k_mean, preferred_element_type=jnp.float32)
        gate = jnp.where(jnp.arange(n_blk) < own, gate, -jnp.inf)
        _, gidx = lax.top_k(gate, topk)
        valid = gidx < own
        ksel = k_blocks[b_idx, h_idx, gidx]
        vsel = v_blocks[b_idx, h_idx, gidx]
        l_sel = jnp.einsum('bhqd,bhqjkd->bhqjk', qc, ksel, preferred_element_type=jnp.float32) * scale
        l_sel = jnp.where(valid[..., None], l_sel, -jnp.inf).reshape(B, H, MOBA_Q_CHUNK, topk * MOBA_BLOCK)
        kown = lax.dynamic_slice_in_dim(kp, own * MOBA_BLOCK, MOBA_BLOCK, axis=2)
        vown = lax.dynamic_slice_in_dim(vp, own * MOBA_BLOCK, MOBA_BLOCK, axis=2)
        l_own = jnp.einsum('bhqd,bhkd->bhqk', qc, kown, preferred_element_type=jnp.float32) * scale
        causal = (q0 + jnp.arange(MOBA_Q_CHUNK))[:, None] >= (own * MOBA_BLOCK + jnp.arange(MOBA_BLOCK))[None, :]
        l_own = jnp.where(causal, l_own, -jnp.inf)
        p = jax.nn.softmax(jnp.concatenate([l_sel, l_own], axis=-1), axis=-1).astype(v.dtype)
        p_sel = p[..., :topk * MOBA_BLOCK].reshape(B, H, MOBA_Q_CHUNK, topk, MOBA_BLOCK)
        p_own = p[..., topk * MOBA_BLOCK:]
        return (jnp.einsum('bhqjk,bhqjkd->bhqd', p_sel, vsel)
                + jnp.einsum('bhqk,bhkd->bhqd', p_own, vown))

    out = lax.map(chunk, jnp.arange(Sp // MOBA_Q_CHUNK))
    return jnp.moveaxis(out, 0, 2).reshape(B, H, Sp, hd)[:, :, :S]


def banded_causal_attention(q, k, v, span):
    *lead, L, hd = q.shape
    nb = -(-L // BAND_BLOCK)
    Lp = nb * BAND_BLOCK
    pad = [(0, 0)] * len(lead) + [(0, Lp - L), (0, 0)]

    def blocks(t):
        return jnp.pad(t, pad).reshape(*lead, nb, BAND_BLOCK, hd)

    def with_prev(t):
        prev = jnp.concatenate([jnp.zeros_like(t[..., :1, :, :]), t[..., :-1, :, :]], axis=-3)
        return jnp.concatenate([prev, t], axis=-2)

    qb = blocks(q)
    kw = with_prev(blocks(k))
    vw = with_prev(blocks(v))
    logits = jnp.einsum('...nqd,...nkd->...nqk', qb, kw, preferred_element_type=jnp.float32) * (hd ** -0.5)
    qpos = (jnp.arange(nb)[:, None] * BAND_BLOCK + jnp.arange(BAND_BLOCK)[None, :])[:, :, None]
    kpos = (jnp.arange(nb)[:, None] * BAND_BLOCK - BAND_BLOCK + jnp.arange(2 * BAND_BLOCK)[None, :])[:, None, :]
    dist = qpos - kpos
    mask = (dist >= 0) & (dist <= span) & (kpos >= 0)
    logits = jnp.where(mask, logits, -jnp.inf)
    lse = jax.nn.logsumexp(logits, axis=-1)
    p = jnp.exp(logits - lse[..., None]).astype(v.dtype)
    out = jnp.einsum('...nqk,...nkd->...nqd', p, vw)
    return out.reshape(*lead, Lp, hd)[..., :L, :], lse.reshape(*lead, Lp)[..., :L]


def dilated_attention(q, k, v):
    B, H, S, hd = q.shape
    outs, lses = [], []
    for window, dil in DIL_PATTERNS:
        L = S // dil

        def to_strided(t):
            return t.reshape(B, H, L, dil, hd).transpose(0, 1, 3, 2, 4)

        o, lse = banded_causal_attention(to_strided(q), to_strided(k), to_strided(v), window // dil)
        outs.append(o.transpose(0, 1, 3, 2, 4).reshape(B, H, S, hd))
        lses.append(lse.transpose(0, 1, 3, 2).reshape(B, H, S))
    w = jax.nn.softmax(jnp.stack(lses, axis=0), axis=0)
    acc = w[0][..., None].astype(q.dtype) * outs[0]
    for i in range(1, len(DIL_PATTERNS)):
        acc = acc + w[i][..., None].astype(q.dtype) * outs[i]
    return acc


def hybrid_mixer(h, w_in, b_f, mix_gain, w_out, cos, sin):
    B, S, _ = h.shape
    proj = h @ w_in
    qkv = proj[..., :3 * MIX_WIDTH].reshape(B, S, 3, N_HEADS, HEAD_DIM).transpose(2, 0, 3, 1, 4)
    q, k, v = qkv[0], qkv[1], qkv[2]
    f_logit = proj[..., 3 * MIX_WIDTH:].astype(jnp.float32) + b_f.astype(jnp.float32)
    log_f = jax.nn.log_sigmoid(f_logit).transpose(0, 2, 1)
    o_a = forgetting_attention(q[:, :H_FOX], k[:, :H_FOX], v[:, :H_FOX], log_f)
    q_r = partial_rope(q[:, H_FOX:], cos, sin)
    k_r = partial_rope(k[:, H_FOX:], cos, sin)
    o_b = moba_attention(q_r[:, :H_MOBA], k_r[:, :H_MOBA], v[:, H_FOX:H_FOX + H_MOBA])
    o_c = dilated_attention(q_r[:, H_MOBA:], k_r[:, H_MOBA:], v[:, H_FOX + H_MOBA:])
    wa, wb = H_FOX * HEAD_DIM, H_MOBA * HEAD_DIM
    ya = rmsnorm(o_a.transpose(0, 2, 1, 3).reshape(B, S, wa), mix_gain[:wa])
    yb = rmsnorm(o_b.transpose(0, 2, 1, 3).reshape(B, S, wb), mix_gain[wa:wa + wb])
    yc = rmsnorm(o_c.transpose(0, 2, 1, 3).reshape(B, S, H_DIL * HEAD_DIM), mix_gain[wa + wb:])
    return jnp.concatenate([ya, yb, yc], axis=-1) @ w_out


def swiglu(h, w1, w3, w2):
    return (jax.nn.silu(h @ w1) * (h @ w3)) @ w2


def moe_swiglu(h, w_router, we1, we3, we2):
    B, S, D = h.shape
    T = B * S
    x = h.reshape(T, D)
    logits = (x @ w_router).astype(jnp.float32)
    top_val, top_idx = lax.top_k(logits, TOP_K)
    gates = jax.nn.softmax(top_val, axis=-1)
    A = T * TOP_K
    expert = top_idx.reshape(A).astype(jnp.int32)
    token = jnp.repeat(jnp.arange(T, dtype=jnp.int32), TOP_K)
    gate = gates.reshape(A)
    order = jnp.argsort(expert)
    e_s, t_s, g_s = expert[order], token[order], gate[order]
    counts = jnp.bincount(expert, length=N_EXPERTS)
    padded = (counts + MOE_ROW_BLOCK - 1) // MOE_ROW_BLOCK * MOE_ROW_BLOCK
    starts = jnp.cumsum(counts) - counts
    pstarts = jnp.cumsum(padded) - padded
    dest = pstarts[e_s] + jnp.arange(A, dtype=jnp.int32) - starts[e_s]
    n_blocks = -(-(A + N_EXPERTS * (MOE_ROW_BLOCK - 1)) // MOE_ROW_BLOCK)
    n_rows = n_blocks * MOE_ROW_BLOCK
    row_token = jnp.full((n_rows,), T, jnp.int32).at[dest].set(t_s)
    row_gate = jnp.zeros((n_rows,), jnp.float32).at[dest].set(g_s)
    block_expert = jnp.minimum(
        jnp.searchsorted(jnp.cumsum(padded), jnp.arange(n_blocks) * MOE_ROW_BLOCK, side='right'),
        N_EXPERTS - 1)
    x_pad = jnp.concatenate([x, jnp.zeros((1, D), x.dtype)], axis=0)

    def body(y, inp):
        tok, g, e = inp
        xb = x_pad[tok]
        hid = jax.nn.silu(xb @ we1[e]) * (xb @ we3[e])
        return y.at[tok].add((hid @ we2[e]) * g[:, None].astype(xb.dtype)), None

    y, _ = lax.scan(body, jnp.zeros_like(x_pad),
                    (row_token.reshape(n_blocks, MOE_ROW_BLOCK),
                     row_gate.reshape(n_blocks, MOE_ROW_BLOCK), block_expert))
    return y[:T].reshape(B, S, D)


def setup_inputs(seed: int = 0) -> dict:
    key = jax.random.key(seed)
    ks = iter(jax.random.split(key, 64))

    def nrm(shape, scale):
        return jax.random.normal(next(ks), shape, jnp.float32) * scale

    def gain(n):
        return 1.0 + nrm((n,), 0.02)

    inp = {"x": nrm((BATCH, SEQ, D_MODEL), 1.0)}
    for l in range(DEPTH):
        p = "l%d_" % l
        inp[p + "norm_mix"] = gain(D_MODEL)
        inp[p + "w_in"] = nrm((D_MODEL, 3 * MIX_WIDTH + H_FOX), D_MODEL ** -0.5)
        inp[p + "b_f"] = FORGET_BIAS_INIT + nrm((H_FOX,), 0.5)
        inp[p + "mix_gain"] = gain(MIX_WIDTH)
        inp[p + "w_out"] = nrm((MIX_WIDTH, D_MODEL), MIX_WIDTH ** -0.5)
        inp[p + "norm_ffn"] = gain(D_MODEL)
        if l % 2 == 0:
            inp[p + "w1"] = nrm((D_MODEL, D_FF), D_MODEL ** -0.5)
            inp[p + "w3"] = nrm((D_MODEL, D_FF), D_MODEL ** -0.5)
            inp[p + "w2"] = nrm((D_FF, D_MODEL), D_FF ** -0.5)
        else:
            inp[p + "w_router"] = nrm((D_MODEL, N_EXPERTS), D_MODEL ** -0.5)
            inp[p + "we1"] = nrm((N_EXPERTS, D_MODEL, D_FF), D_MODEL ** -0.5)
            inp[p + "we3"] = nrm((N_EXPERTS, D_MODEL, D_FF), D_MODEL ** -0.5)
            inp[p + "we2"] = nrm((N_EXPERTS, D_FF, D_MODEL), D_FF ** -0.5)
    inp["final_norm"] = gain(D_MODEL)
    return inp


def reference(x, l0_norm_mix, l0_w_in, l0_b_f, l0_mix_gain, l0_w_out, l0_norm_ffn, l0_w1, l0_w3, l0_w2,
              l1_norm_mix, l1_w_in, l1_b_f, l1_mix_gain, l1_w_out, l1_norm_ffn,
              l1_w_router, l1_we1, l1_we3, l1_we2, final_norm):
    cos, sin = rope_tables(x.shape[1])
    mix_params = [(l0_norm_mix, l0_w_in, l0_b_f, l0_mix_gain, l0_w_out),
                  (l1_norm_mix, l1_w_in, l1_b_f, l1_mix_gain, l1_w_out)]
    ffn_norms = [l0_norm_ffn, l1_norm_ffn]
    ffn_params = [(l0_w1, l0_w3, l0_w2), (l1_w_router, l1_we1, l1_we3, l1_we2)]
    for i in range(DEPTH):
        g_mix, w_in, b_f, mix_gain, w_out = mix_params[i]
        x = x + hybrid_mixer(rmsnorm(x, g_mix), w_in, b_f, mix_gain, w_out, cos, sin)
        h = rmsnorm(x, ffn_norms[i])
        x = x + (swiglu(h, *ffn_params[i]) if i % 2 == 0 else moe_swiglu(h, *ffn_params[i]))
    return rmsnorm(x, final_norm)
```

```python
import functools
import math

import numpy as np
import jax
import jax.numpy as jnp
from jax import lax
from jax.experimental import pallas as pl
from jax.experimental.pallas import tpu as pltpu

D_MODEL = 1024
SEQ = 2048
HEAD_DIM = 64
N_HEADS = 16
H_FOX = 5
H_MOBA = 5
H_DIL = 6
ROT_DIM = 16
ROPE_THETA = 500000.0
MOBA_BLOCK = 256
MOBA_TOPK = 3
DIL_PATTERNS = ((128, 1), (512, 4), (2048, 16))
D_FF = 3584
N_EXPERTS = 8
EPS = 1e-6

LANE = 128
PAIR = 2 * HEAD_DIM
QKV_W = 3 * N_HEADS * HEAD_DIM
ATT_BLK = 256
N_ATT_BLK = SEQ // ATT_BLK
NEG = -1e30
VMEM_LIMIT = 56 * 1024 * 1024

BF16 = jnp.bfloat16
F32 = jnp.float32


def _dot(a, b):
    return jnp.dot(a, b, preferred_element_type=F32)


def _dot_nt(a, b):
    return lax.dot_general(a, b, (((1,), (1,)), ((), ())), preferred_element_type=F32)


def _split3(x):
    hi = x.astype(BF16)
    r1 = x - hi.astype(F32)
    mid = r1.astype(BF16)
    lo = (r1 - mid.astype(F32)).astype(BF16)
    return hi, mid, lo


def _rms(x, g):
    return x * lax.rsqrt(jnp.mean(x * x, axis=-1, keepdims=True) + EPS) * g


def _cparams(sem):
    return pltpu.CompilerParams(dimension_semantics=sem, vmem_limit_bytes=VMEM_LIMIT)


IN_TM = 512
IN_TN = 512


def _inproj_kernel(x_ref, g_ref, w_ref, wf_ref, cos_ref, sp_ref, sm_ref, qkv_ref, f_ref):
    hb = _rms(x_ref[...], g_ref[...]).astype(BF16)
    f_ref[...] = _dot(hb, wf_ref[...])
    cos, sp, sm = cos_ref[...], sp_ref[...], sm_ref[...]
    lane = lax.broadcasted_iota(jnp.int32, (1, LANE), 1)
    upper = lane >= HEAD_DIM
    n_pairs = N_HEADS // 2
    first_rot_head = H_FOX
    for c in range(QKV_W // IN_TN):
        y = _dot(hb, w_ref[:, c * IN_TN:(c + 1) * IN_TN])
        for t in range(IN_TN // LANE):
            col = c * (IN_TN // LANE) + t
            part, pair = divmod(col, n_pairs)
            yt = y[:, t * LANE:(t + 1) * LANE]
            lo_head, hi_head = 2 * pair, 2 * pair + 1
            if part < 2 and hi_head >= first_rot_head:
                c_t, sp_t, sm_t = cos, sp, sm
                if lo_head < first_rot_head:
                    c_t = jnp.where(upper, cos, 1.0)
                    sp_t = jnp.where(upper, sp, 0.0)
                    sm_t = jnp.where(upper, sm, 0.0)
                half = ROT_DIM // 2
                yt = (yt * c_t + pltpu.roll(yt, half, axis=1) * sp_t
                      + pltpu.roll(yt, LANE - half, axis=1) * sm_t)
            qkv_ref[:, col * LANE:(col + 1) * LANE] = yt.astype(BF16)


def _inproj(x2, g, w_qkv, w_f, cos_t, sp_t, sm_t):
    T = x2.shape[0]
    n_pos = SEQ // IN_TM
    tab = pl.BlockSpec((IN_TM, LANE), lambda i: (i % n_pos, 0))
    return pl.pallas_call(
        _inproj_kernel,
        grid=(T // IN_TM,),
        in_specs=[
            pl.BlockSpec((IN_TM, D_MODEL), lambda i: (i, 0)),
            pl.BlockSpec((1, D_MODEL), lambda i: (0, 0)),
            pl.BlockSpec((D_MODEL, QKV_W), lambda i: (0, 0)),
            pl.BlockSpec((D_MODEL, LANE), lambda i: (0, 0)),
            tab, tab, tab,
        ],
        out_specs=[
            pl.BlockSpec((IN_TM, QKV_W), lambda i: (i, 0)),
            pl.BlockSpec((IN_TM, LANE), lambda i: (i, 0)),
        ],
        out_shape=[
            jax.ShapeDtypeStruct((T, QKV_W), BF16),
            jax.ShapeDtypeStruct((T, LANE), F32),
        ],
        compiler_params=_cparams(("parallel",)),
        name="inproj",
    )(x2, g, w_qkv, w_f, cos_t, sp_t, sm_t)


def _cumsum_rows(x, tri, exact_small_ints):
    n = x.shape[0]
    carry = jnp.zeros((1, LANE), F32)
    out = []
    for b in range(n // LANE):
        xb = x[b * LANE:(b + 1) * LANE]
        if exact_small_ints:
            y = _dot(tri, xb.astype(BF16))
        else:
            hi, mid, lo = _split3(xb)
            y = _dot(tri, hi) + _dot(tri, mid) + _dot(tri, lo)
        y = y + carry
        carry = y[LANE - 1:LANE, :]
        out.append(y)
    return out


def _forget_kernel(f_ref, bf_ref, tri_ref, ccol_ref, crow_ref):
    z = f_ref[0] + bf_ref[...]
    logf = jnp.minimum(z, 0.0) - jnp.log(1.0 + jnp.exp(-jnp.abs(z)))
    blocks = _cumsum_rows(logf, tri_ref[...], exact_small_ints=False)
    for b, y in enumerate(blocks):
        ccol_ref[0, b * LANE:(b + 1) * LANE, :] = y
        crow_ref[0, :, b * LANE:(b + 1) * LANE] = y.T[:8, :]


def _forget_cumsum(f3, bf_pad, tri):
    B = f3.shape[0]
    return pl.pallas_call(
        _forget_kernel,
        grid=(B,),
        in_specs=[
            pl.BlockSpec((1, SEQ, LANE), lambda b: (b, 0, 0)),
            pl.BlockSpec((1, LANE), lambda b: (0, 0)),
            pl.BlockSpec((LANE, LANE), lambda b: (0, 0)),
        ],
        out_specs=[
            pl.BlockSpec((1, SEQ, LANE), lambda b: (b, 0, 0)),
            pl.BlockSpec((1, 8, SEQ), lambda b: (b, 0, 0)),
        ],
        out_shape=[
            jax.ShapeDtypeStruct((B, SEQ, LANE), F32),
            jax.ShapeDtypeStruct((B, 8, SEQ), F32),
        ],
        compiler_params=_cparams(("parallel",)),
        name="forget_cumsum",
    )(f3, bf_pad, tri)


def _online_step(state, s, v_blk):
    m, l, acc = state
    m_new = jnp.maximum(m, jnp.max(s, axis=-1, keepdims=True))
    p = jnp.exp(s - m_new)
    alpha = jnp.exp(m - m_new)
    l = alpha * l + jnp.sum(p, axis=-1, keepdims=True)
    acc = alpha * acc + _dot(p.astype(BF16), v_blk)
    return m_new, l, acc


def _attn_kernel(mode, h0, q_ref, k_ref, v_ref, *rest):
    if mode == "fox":
        ccol_ref, crow_ref, o_ref = rest
    elif mode == "moba":
        o_ref, km_scr, sel_scr = rest
    else:
        lm_ref, o_ref = rest
    j = pl.program_id(1)
    h = h0 + j
    off = (h % 2) * HEAD_DIM
    lane = lax.broadcasted_iota(jnp.int32, (1, LANE), 1)
    hmask = (lane >= off) & (lane < off + HEAD_DIM)
    first_visit = (j == 0) | (h % 2 == 0)
    scale = HEAD_DIM ** -0.5

    rows = lax.broadcasted_iota(jnp.int32, (ATT_BLK, ATT_BLK), 0)
    cols = lax.broadcasted_iota(jnp.int32, (ATT_BLK, ATT_BLK), 1)
    causal = rows >= cols

    if mode == "moba":
        km_scr[...] = jnp.zeros((LANE, LANE), F32)
        for n in range(N_ATT_BLK):
            kb = k_ref[0, n * ATT_BLK:(n + 1) * ATT_BLK, :].astype(F32)
            km = jnp.sum(kb, axis=0, keepdims=True) * (1.0 / MOBA_BLOCK)
            km_scr[n:n + 1, :] = jnp.where(hmask, km, 0.0)
        km_hi, km_mid, km_lo = _split3(km_scr[...])
        sub8 = lax.broadcasted_iota(jnp.int32, (8, ATT_BLK), 0)

    for i in range(N_ATT_BLK):
        r0 = i * ATT_BLK
        q_blk = q_ref[0, r0:r0 + ATT_BLK, :]
        qm = jnp.where(hmask, q_blk * scale, 0.0).astype(BF16)

        if mode == "fox":
            ct = jnp.sum(jnp.where(lane == h, ccol_ref[0, r0:r0 + ATT_BLK, :], 0.0),
                         axis=1, keepdims=True)
        if mode == "moba" and i > MOBA_TOPK:
            gt = (_dot_nt(km_hi, qm) + _dot_nt(km_mid, qm) + _dot_nt(km_lo, qm))[:8, :]
            rank = jnp.zeros((8, ATT_BLK), F32)
            for n2 in range(i):
                g2 = gt[n2:n2 + 1, :]
                ahead = (g2 > gt) | ((g2 == gt) & (n2 < sub8))
                rank = rank + jnp.where(ahead, 1.0, 0.0)
            sel_t = jnp.where((rank < MOBA_TOPK) & (sub8 < i), 1.0, 0.0)
            sel_scr[...] = jnp.zeros((LANE, ATT_BLK), F32)
            sel_scr[0:8, :] = sel_t
            sel = sel_scr[...].T

        k_blk = k_ref[0, r0:r0 + ATT_BLK, :]
        v_blk = v_ref[0, r0:r0 + ATT_BLK, :]
        s = _dot_nt(qm, k_blk)
        if mode == "fox":
            s = s + (ct - crow_ref[0, pl.ds(h, 1), r0:r0 + ATT_BLK])
            s = jnp.where(causal, s, NEG)
        elif mode == "moba":
            s = jnp.where(causal, s, NEG)
        else:
            s = s + lm_ref[0]
        state = (jnp.full((ATT_BLK, 1), NEG, F32), jnp.zeros((ATT_BLK, 1), F32),
                 jnp.zeros((ATT_BLK, LANE), F32))
        state = _online_step(state, s, v_blk)

        for n in range(i):
            c0 = n * ATT_BLK
            k_blk = k_ref[0, c0:c0 + ATT_BLK, :]
            v_blk = v_ref[0, c0:c0 + ATT_BLK, :]
            s = _dot_nt(qm, k_blk)
            if mode == "fox":
                s = s + (ct - crow_ref[0, pl.ds(h, 1), c0:c0 + ATT_BLK])
            elif mode == "moba":
                if i > MOBA_TOPK:
                    s = jnp.where(sel[:, n:n + 1] > 0.5, s, NEG)
            else:
                s = s + lm_ref[i - n]
            state = _online_step(state, s, v_blk)

        _, l, acc = state
        o = jnp.where(hmask, acc * (1.0 / l), 0.0)

        @pl.when(first_visit)
        def _():
            o_ref[0, r0:r0 + ATT_BLK, :] = o

        @pl.when(jnp.logical_not(first_visit))
        def _():
            o_ref[0, r0:r0 + ATT_BLK, :] = o_ref[0, r0:r0 + ATT_BLK, :] + o


def _attention(mode, h0, n_heads, qkv3, extras):
    B = qkv3.shape[0]
    n_pairs = N_HEADS // 2
    pair0 = h0 // 2
    n_out_pairs = (h0 + n_heads - 1) // 2 - pair0 + 1

    def col(part):
        return lambda b, j: (b, 0, part * n_pairs + (h0 + j) // 2)

    in_specs = [pl.BlockSpec((1, SEQ, LANE), col(p)) for p in range(3)]
    scratch = []
    if mode == "fox":
        in_specs += [pl.BlockSpec((1, SEQ, LANE), lambda b, j: (b, 0, 0)),
                     pl.BlockSpec((1, 8, SEQ), lambda b, j: (b, 0, 0))]
    elif mode == "moba":
        scratch = [pltpu.VMEM((LANE, LANE), F32), pltpu.VMEM((LANE, ATT_BLK), F32)]
    else:
        in_specs += [pl.BlockSpec((N_ATT_BLK, ATT_BLK, ATT_BLK), lambda b, j: (0, 0, 0))]
    return pl.pallas_call(
        functools.partial(_attn_kernel, mode, h0),
        grid=(B, n_heads),
        in_specs=in_specs,
        out_specs=pl.BlockSpec((1, SEQ, LANE), lambda b, j: (b, 0, (h0 + j) // 2 - pair0)),
        out_shape=jax.ShapeDtypeStruct((B, SEQ, n_out_pairs * LANE), F32),
        scratch_shapes=scratch,
        compiler_params=_cparams(("parallel", "arbitrary")),
        name="attn_" + mode,
    )(qkv3, qkv3, qkv3, *extras)


def _dilated_log_multiplicity():
    r = np.arange(ATT_BLK)[:, None]
    c = np.arange(ATT_BLK)[None, :]
    out = np.empty((N_ATT_BLK, ATT_BLK, ATT_BLK), np.float32)
    for d in range(N_ATT_BLK):
        dist = d * ATT_BLK + r - c
        mult = np.zeros_like(dist)
        for window, dil in DIL_PATTERNS:
            mult += ((dist >= 0) & (dist % dil == 0) & (dist // dil <= window // dil)).astype(dist.dtype)
        out[d] = np.where(mult > 0, np.log(np.maximum(mult, 1)), NEG)
    return out


OUT_TM = 512
MIX_W = 3 * LANE


def _outproj_kernel(x_ref, oa_ref, ob_ref, oc_ref, ga_ref, gb_ref, gc_ref, wa_ref, wb_ref, wc_ref, o_ref):
    def normed(o_r, g_r, width):
        o = o_r[...]
        ms = jnp.sum(o * o, axis=-1, keepdims=True) * (1.0 / width)
        return (o * lax.rsqrt(ms + EPS) * g_r[...]).astype(BF16)

    ya = normed(oa_ref, ga_ref, H_FOX * HEAD_DIM)
    yb = normed(ob_ref, gb_ref, H_MOBA * HEAD_DIM)
    yc = normed(oc_ref, gc_ref, H_DIL * HEAD_DIM)
    o_ref[...] = x_ref[...] + _dot(ya, wa_ref[...]) + _dot(yb, wb_ref[...]) + _dot(yc, wc_ref[...])


def _outproj(x2, oa, ob, oc, ga, gb, gc, wa, wb, wc):
    T = x2.shape[0]
    row = lambda w: pl.BlockSpec((OUT_TM, w), lambda i: (i, 0))
    full = lambda a, b: pl.BlockSpec((a, b), lambda i: (0, 0))
    return pl.pallas_call(
        _outproj_kernel,
        grid=(T // OUT_TM,),
        in_specs=[row(D_MODEL), row(MIX_W), row(MIX_W), row(MIX_W),
                  full(1, MIX_W), full(1, MIX_W), full(1, MIX_W),
                  full(MIX_W, D_MODEL), full(MIX_W, D_MODEL), full(MIX_W, D_MODEL)],
        out_specs=row(D_MODEL),
        out_shape=jax.ShapeDtypeStruct((T, D_MODEL), F32),
        compiler_params=_cparams(("parallel",)),
        name="outproj",
    )(x2, oa, ob, oc, ga, gb, gc, wa, wb, wc)


FFN_TM = 1024
FFN_TF = 512


def _swiglu_act(a, b):
    return a * (1.0 / (1.0 + jnp.exp(-a))) * b


def _ffn_kernel(x_ref, g_ref, w1_ref, w3_ref, w2_ref, o_ref, h_scr, acc_scr):
    f = pl.program_id(1)

    @pl.when(f == 0)
    def _():
        h_scr[...] = _rms(x_ref[...], g_ref[...]).astype(BF16)

    h = h_scr[...]
    act = _swiglu_act(_dot(h, w1_ref[...]), _dot(h, w3_ref[...])).astype(BF16)
    y = _dot(act, w2_ref[...])

    @pl.when(f == 0)
    def _():
        acc_scr[...] = y

    @pl.when(f > 0)
    def _():
        acc_scr[...] = acc_scr[...] + y

    @pl.when(f == pl.num_programs(1) - 1)
    def _():
        o_ref[...] = x_ref[...] + acc_scr[...]


def _ffn(x2, g, w1, w3, w2):
    T = x2.shape[0]
    return pl.pallas_call(
        _ffn_kernel,
        grid=(T // FFN_TM, D_FF // FFN_TF),
        in_specs=[
            pl.BlockSpec((FFN_TM, D_MODEL), lambda i, f: (i, 0)),
            pl.BlockSpec((1, D_MODEL), lambda i, f: (0, 0)),
            pl.BlockSpec((D_MODEL, FFN_TF), lambda i, f: (0, f)),
            pl.BlockSpec((D_MODEL, FFN_TF), lambda i, f: (0, f)),
            pl.BlockSpec((FFN_TF, D_MODEL), lambda i, f: (f, 0)),
        ],
        out_specs=pl.BlockSpec((FFN_TM, D_MODEL), lambda i, f: (i, 0)),
        out_shape=jax.ShapeDtypeStruct((T, D_MODEL), F32),
        scratch_shapes=[pltpu.VMEM((FFN_TM, D_MODEL), BF16), pltpu.VMEM((FFN_TM, D_MODEL), F32)],
        compiler_params=_cparams(("parallel", "arbitrary")),
        name="ffn_dense",
    )(x2, g, w1, w3, w2)


MOE_TT = 2048
MOE_CH = 128
MOE_TF = 512
MOE_SB = 256


def _router_kernel(x_ref, g_ref, wr_hi_ref, wr_lo_ref, tri_ref,
                   h_ref, posc_ref, gatec_ref, posr_ref, cnt_ref):
    h = _rms(x_ref[...], g_ref[...])
    h_hi = h.astype(BF16)
    h_lo = (h - h_hi.astype(F32)).astype(BF16)
    h_ref[...] = h_hi
    wr_hi, wr_lo = wr_hi_ref[...], wr_lo_ref[...]
    logits = _dot(h_hi, wr_hi) + _dot(h_lo, wr_hi) + _dot(h_hi, wr_lo)
    lane = lax.broadcasted_iota(jnp.int32, (1, LANE), 1).astype(F32)
    lg = jnp.where(lane < N_EXPERTS, logits, -jnp.inf)
    m1 = jnp.max(lg, axis=1, keepdims=True)
    i1 = jnp.min(jnp.where(lg == m1, lane, float(LANE)), axis=1, keepdims=True)
    lg2 = jnp.where(lane == i1, -jnp.inf, lg)
    m2 = jnp.max(lg2, axis=1, keepdims=True)
    i2 = jnp.min(jnp.where(lg2 == m2, lane, float(LANE)), axis=1, keepdims=True)
    e2 = jnp.exp(m2 - m1)
    g1 = 1.0 / (1.0 + e2)
    g2 = e2 * g1
    routed = (lane == i1) | (lane == i2)
    gatec_ref[...] = jnp.where(lane == i1, g1, jnp.where(lane == i2, g2, 0.0))
    routed_f = jnp.where(routed, 1.0, 0.0)
    blocks = _cumsum_rows(routed_f, tri_ref[...], exact_small_ints=True)
    for b, cnt in enumerate(blocks):
        sl = slice(b * LANE, (b + 1) * LANE)
        pos = jnp.where(routed_f[sl] > 0.5, cnt - 1.0, -1.0)
        posc_ref[sl, :] = pos
        posr_ref[0, :, sl] = pos.T[:8, :]
    cnt_ref[0] = blocks[-1][LANE - 1:LANE, :]


def _router(x2, g, wr_hi, wr_lo, tri):
    T = x2.shape[0]
    nt = T // MOE_TT
    return pl.pallas_call(
        _router_kernel,
        grid=(nt,),
        in_specs=[
            pl.BlockSpec((MOE_TT, D_MODEL), lambda i: (i, 0)),
            pl.BlockSpec((1, D_MODEL), lambda i: (0, 0)),
            pl.BlockSpec((D_MODEL, LANE), lambda i: (0, 0)),
            pl.BlockSpec((D_MODEL, LANE), lambda i: (0, 0)),
            pl.BlockSpec((LANE, LANE), lambda i: (0, 0)),
        ],
        out_specs=[
            pl.BlockSpec((MOE_TT, D_MODEL), lambda i: (i, 0)),
            pl.BlockSpec((MOE_TT, LANE), lambda i: (i, 0)),
            pl.BlockSpec((MOE_TT, LANE), lambda i: (i, 0)),
            pl.BlockSpec((1, 8, MOE_TT), lambda i: (i, 0, 0)),
            pl.BlockSpec((1, 1, LANE), lambda i: (i, 0, 0)),
        ],
        out_shape=[
            jax.ShapeDtypeStruct((T, D_MODEL), BF16),
            jax.ShapeDtypeStruct((T, LANE), F32),
            jax.ShapeDtypeStruct((T, LANE), F32),
            jax.ShapeDtypeStruct((nt, 8, MOE_TT), F32),
            jax.ShapeDtypeStruct((nt, 1, LANE), F32),
        ],
        compiler_params=_cparams(("parallel",)),
        name="moe_router",
    )(x2, g, wr_hi, wr_lo, tri)


def _moe_kernel(cnt_ref, h_ref, posc_ref, gatec_ref, posr_ref, w1_ref, w3_ref, w2_ref, o_ref,
                xe_scr, ye_scr, pe_scr, gs_scr):
    t, e, f = pl.program_id(0), pl.program_id(1), pl.program_id(2)
    nf = pl.num_programs(2)
    n_rows = cnt_ref[t * N_EXPERTS + e]
    n_chunks = (n_rows + MOE_CH - 1) // MOE_CH
    lane = lax.broadcasted_iota(jnp.int32, (1, LANE), 1)

    @pl.when((e == 0) & (f == 0))
    def _():
        o_ref[...] = jnp.zeros((MOE_TT, D_MODEL), F32)

    @pl.when(f == 0)
    def _():
        is_e = lane == e
        pe_scr[...] = jnp.sum(jnp.where(is_e, posc_ref[...], 0.0), axis=1, keepdims=True)
        ge = jnp.sum(jnp.where(is_e, gatec_ref[...], 0.0), axis=1, keepdims=True)
        g_hi, g_mid, g_lo = _split3(jnp.broadcast_to(ge, (MOE_TT, LANE)))
        pos_row = posr_ref[0, pl.ds(e, 1), :]
        slot = lax.broadcasted_iota(jnp.int32, (MOE_CH, 1), 0).astype(F32)

        def gather(c, _):
            r0 = pl.multiple_of(c * MOE_CH, MOE_CH)
            onehot = jnp.where(pos_row == slot + (c * MOE_CH).astype(F32), 1.0, 0.0).astype(BF16)
            xe_scr[pl.ds(r0, MOE_CH), :] = _dot(onehot, h_ref[...]).astype(BF16)
            gs_scr[pl.ds(r0, MOE_CH), :] = _dot(onehot, g_hi) + _dot(onehot, g_mid) + _dot(onehot, g_lo)
            return 0

        lax.fori_loop(0, n_chunks, gather, 0)

    def expert(c, _):
        r0 = pl.multiple_of(c * MOE_CH, MOE_CH)
        xc = xe_scr[pl.ds(r0, MOE_CH), :]
        act = _swiglu_act(_dot(xc, w1_ref[0]), _dot(xc, w3_ref[0])).astype(BF16)
        y = _dot(act, w2_ref[0])

        @pl.when(f == 0)
        def _():
            ye_scr[pl.ds(r0, MOE_CH), :] = y

        @pl.when(f > 0)
        def _():
            ye_scr[pl.ds(r0, MOE_CH), :] = ye_scr[pl.ds(r0, MOE_CH), :] + y
        return 0

    lax.fori_loop(0, n_chunks, expert, 0)

    @pl.when(f == nf - 1)
    def _():
        slot_l = lax.broadcasted_iota(jnp.int32, (1, MOE_CH), 1).astype(F32)

        def scatter(c, _):
            r0 = pl.multiple_of(c * MOE_CH, MOE_CH)
            yc = (ye_scr[pl.ds(r0, MOE_CH), :] * gs_scr[pl.ds(r0, MOE_CH), 0:1]).astype(BF16)
            slots = slot_l + (c * MOE_CH).astype(F32)
            for tb in range(MOE_TT // MOE_SB):
                rows = slice(tb * MOE_SB, (tb + 1) * MOE_SB)
                onehot_t = jnp.where(pe_scr[rows, :] == slots, 1.0, 0.0).astype(BF16)
                o_ref[rows, :] = o_ref[rows, :] + _dot(onehot_t, yc)
            return 0

        lax.fori_loop(0, n_chunks, scatter, 0)


def _moe(counts, h, posc, gatec, posr, we1, we3, we2):
    T = h.shape[0]
    nt = T // MOE_TT
    grid_spec = pltpu.PrefetchScalarGridSpec(
        num_scalar_prefetch=1,
        grid=(nt, N_EXPERTS, D_FF // MOE_TF),
        in_specs=[
            pl.BlockSpec((MOE_TT, D_MODEL), lambda t, e, f, c: (t, 0)),
            pl.BlockSpec((MOE_TT, LANE), lambda t, e, f, c: (t, 0)),
            pl.BlockSpec((MOE_TT, LANE), lambda t, e, f, c: (t, 0)),
            pl.BlockSpec((1, 8, MOE_TT), lambda t, e, f, c: (t, 0, 0)),
            pl.BlockSpec((1, D_MODEL, MOE_TF), lambda t, e, f, c: (e, 0, f)),
            pl.BlockSpec((1, D_MODEL, MOE_TF), lambda t, e, f, c: (e, 0, f)),
            pl.BlockSpec((1, MOE_TF, D_MODEL), lambda t, e, f, c: (e, f, 0)),
        ],
        out_specs=pl.BlockSpec((MOE_TT, D_MODEL), lambda t, e, f, c: (t, 0)),
        scratch_shapes=[
            pltpu.VMEM((MOE_TT, D_MODEL), BF16),
            pltpu.VMEM((MOE_TT, D_MODEL), F32),
            pltpu.VMEM((MOE_TT, 1), F32),
            pltpu.VMEM((MOE_TT, LANE), F32),
        ],
    )
    return pl.pallas_call(
        _moe_kernel,
        grid_spec=grid_spec,
        out_shape=jax.ShapeDtypeStruct((T, D_MODEL), F32),
        compiler_params=_cparams(("parallel", "arbitrary", "arbitrary")),
        name="moe_experts",
    )(counts, h, posc, gatec, posr, we1, we3, we2)


FIN_TM = 1024


def _final_kernel(x_ref, y_ref, g_ref, o_ref):
    o_ref[...] = _rms(x_ref[...] + y_ref[...], g_ref[...])


def _final(x2, y2, g):
    T = x2.shape[0]
    row = pl.BlockSpec((FIN_TM, D_MODEL), lambda i: (i, 0))
    return pl.pallas_call(
        _final_kernel,
        grid=(T // FIN_TM,),
        in_specs=[row, row, pl.BlockSpec((1, D_MODEL), lambda i: (0, 0))],
        out_specs=row,
        out_shape=jax.ShapeDtypeStruct((T, D_MODEL), F32),
        compiler_params=_cparams(("parallel",)),
        name="final_norm",
    )(x2, y2, g)


def _rope_lane_tables():
    half = ROT_DIM // 2
    inv = ROPE_THETA ** (-jnp.arange(0, ROT_DIM, 2, dtype=F32) / ROT_DIM)
    ang = jnp.arange(SEQ, dtype=F32)[:, None] * inv[None, :]
    cos, sin = jnp.cos(ang), jnp.sin(ang)
    ones = jnp.ones((SEQ, HEAD_DIM - ROT_DIM), F32)
    zeros = jnp.zeros((SEQ, HEAD_DIM - ROT_DIM), F32)
    z8 = jnp.zeros((SEQ, half), F32)
    cos_h = jnp.concatenate([cos, cos, ones], axis=1)
    sp_h = jnp.concatenate([z8, sin, zeros], axis=1)
    sm_h = jnp.concatenate([-sin, z8, zeros], axis=1)
    two = lambda a: jnp.concatenate([a, a], axis=1)
    return two(cos_h), two(sp_h), two(sm_h)


def _pad_lanes(a, width=LANE):
    return jnp.pad(a, ((0, 0), (0, width - a.shape[1])))


def _mixer(x2, B, norm_mix, w_in, b_f, mix_gain, w_out, tables, tri, lm):
    w_qkv = w_in[:, :QKV_W].astype(BF16)
    w_f = _pad_lanes(w_in[:, QKV_W:]).astype(BF16)
    qkv, f_logit = _inproj(x2, norm_mix[None, :], w_qkv, w_f, *tables)
    ccol, crow = _forget_cumsum(f_logit.reshape(B, SEQ, LANE), _pad_lanes(b_f[None, :]), tri)
    qkv3 = qkv.reshape(B, SEQ, QKV_W)
    oa = _attention("fox", 0, H_FOX, qkv3, (ccol, crow))
    ob = _attention("moba", H_FOX, H_MOBA, qkv3, ())
    oc = _attention("dil", H_FOX + H_MOBA, H_DIL, qkv3, (lm,))
    a0, b0, c0 = 0, (H_FOX // 2) * PAIR, ((H_FOX + H_MOBA) // 2) * PAIR
    T = x2.shape[0]
    sl = lambda v, s: v[s:s + MIX_W]
    return _outproj(
        x2, oa.reshape(T, MIX_W), ob.reshape(T, MIX_W), oc.reshape(T, MIX_W),
        sl(mix_gain, a0)[None, :], sl(mix_gain, b0)[None, :], sl(mix_gain, c0)[None, :],
        sl(w_out, a0).astype(BF16), sl(w_out, b0).astype(BF16), sl(w_out, c0).astype(BF16))


def kernel(x, l0_norm_mix, l0_w_in, l0_b_f, l0_mix_gain, l0_w_out, l0_norm_ffn, l0_w1, l0_w3, l0_w2, l1_norm_mix, l1_w_in, l1_b_f, l1_mix_gain, l1_w_out, l1_norm_ffn, l1_w_router, l1_we1, l1_we3, l1_we2, final_norm):
    B, S, D = x.shape
    assert (S, D) == (SEQ, D_MODEL)
    T = B * S
    tables = _rope_lane_tables()
    tri = jnp.asarray(np.tril(np.ones((LANE, LANE), np.float32)), BF16)
    lm = jnp.asarray(_dilated_log_multiplicity())

    x2 = x.reshape(T, D)
    x2 = _mixer(x2, B, l0_norm_mix, l0_w_in, l0_b_f, l0_mix_gain, l0_w_out, tables, tri, lm)
    x2 = _ffn(x2, l0_norm_ffn[None, :], l0_w1.astype(BF16), l0_w3.astype(BF16), l0_w2.astype(BF16))
    x2 = _mixer(x2, B, l1_norm_mix, l1_w_in, l1_b_f, l1_mix_gain, l1_w_out, tables, tri, lm)

    wr = _pad_lanes(l1_w_router)
    wr_hi = wr.astype(BF16)
    wr_lo = (wr - wr_hi.astype(F32)).astype(BF16)
    h, posc, gatec, posr, cnt = _router(x2, l1_norm_ffn[None, :], wr_hi, wr_lo, tri)
    counts = cnt[:, 0, :N_EXPERTS].astype(jnp.int32).reshape(-1)
    y2 = _moe(counts, h, posc, gatec, posr,
              l1_we1.astype(BF16), l1_we3.astype(BF16), l1_we2.astype(BF16))
    return _final(x2, y2, final_norm[None, :]).reshape(B, S, D)
```

```python
import functools
import math

import numpy as np
import jax
import jax.numpy as jnp
from jax import lax
from jax.experimental import pallas as pl
from jax.experimental.pallas import tpu as pltpu

D_MODEL = 1024
SEQ = 2048
HEAD_DIM = 64
N_HEADS = 16
H_FOX = 5
H_MOBA = 5
H_DIL = 6
ROT_DIM = 16
ROPE_THETA = 500000.0
MOBA_BLOCK = 256
MOBA_TOPK = 3
DIL_PATTERNS = ((128, 1), (512, 4), (2048, 16))
D_FF = 3584
N_EXPERTS = 8
EPS = 1e-6

LANE = 128
PAIR = 2 * HEAD_DIM
QKV_W = 3 * N_HEADS * HEAD_DIM
ATT_BLK = 256
N_ATT_BLK = SEQ // ATT_BLK
NEG = -1e30
VMEM_LIMIT = 56 * 1024 * 1024

BF16 = jnp.bfloat16
F32 = jnp.float32


def _dot(a, b):
    return jnp.dot(a, b, preferred_element_type=F32)


def _dot_nt(a, b):
    return lax.dot_general(a, b, (((1,), (1,)), ((), ())), preferred_element_type=F32)


def _split3(x):
    hi = x.astype(BF16)
    r1 = x - hi.astype(F32)
    mid = r1.astype(BF16)
    lo = (r1 - mid.astype(F32)).astype(BF16)
    return hi, mid, lo


def _rms(x, g):
    return x * lax.rsqrt(jnp.mean(x * x, axis=-1, keepdims=True) + EPS) * g


def _cparams(sem):
    return pltpu.CompilerParams(dimension_semantics=sem, vmem_limit_bytes=VMEM_LIMIT)


IN_TM = 512
IN_TN = 512


def _inproj_kernel(x_ref, g_ref, w_ref, wf_ref, cos_ref, sp_ref, sm_ref, qkv_ref, f_ref):
    hb = _rms(x_ref[...], g_ref[...]).astype(BF16)
    f_ref[...] = _dot(hb, wf_ref[...])
    cos, sp, sm = cos_ref[...], sp_ref[...], sm_ref[...]
    lane = lax.broadcasted_iota(jnp.int32, (1, LANE), 1)
    upper = lane >= HEAD_DIM
    n_pairs = N_HEADS // 2
    first_rot_head = H_FOX
    for c in range(QKV_W // IN_TN):
        y = _dot(hb, w_ref[:, c * IN_TN:(c + 1) * IN_TN])
        for t in range(IN_TN // LANE):
            col = c * (IN_TN // LANE) + t
            part, pair = divmod(col, n_pairs)
            yt = y[:, t * LANE:(t + 1) * LANE]
            lo_head, hi_head = 2 * pair, 2 * pair + 1
            if part < 2 and hi_head >= first_rot_head:
                c_t, sp_t, sm_t = cos, sp, sm
                if lo_head < first_rot_head:
                    c_t = jnp.where(upper, cos, 1.0)
                    sp_t = jnp.where(upper, sp, 0.0)
                    sm_t = jnp.where(upper, sm, 0.0)
                half = ROT_DIM // 2
                yt = (yt * c_t + pltpu.roll(yt, half, axis=1) * sp_t
                      + pltpu.roll(yt, LANE - half, axis=1) * sm_t)
            qkv_ref[:, col * LANE:(col + 1) * LANE] = yt.astype(BF16)


def _inproj(x2, g, w_qkv, w_f, cos_t, sp_t, sm_t):
    T = x2.shape[0]
    n_pos = SEQ // IN_TM
    tab = pl.BlockSpec((IN_TM, LANE), lambda i: (i % n_pos, 0))
    return pl.pallas_call(
        _inproj_kernel,
        grid=(T // IN_TM,),
        in_specs=[
            pl.BlockSpec((IN_TM, D_MODEL), lambda i: (i, 0)),
            pl.BlockSpec((1, D_MODEL), lambda i: (0, 0)),
            pl.BlockSpec((D_MODEL, QKV_W), lambda i: (0, 0)),
            pl.BlockSpec((D_MODEL, LANE), lambda i: (0, 0)),
            tab, tab, tab,
        ],
        out_specs=[
            pl.BlockSpec((IN_TM, QKV_W), lambda i: (i, 0)),
            pl.BlockSpec((IN_TM, LANE), lambda i: (i, 0)),
        ],
        out_shape=[
            jax.ShapeDtypeStruct((T, QKV_W), BF16),
            jax.ShapeDtypeStruct((T, LANE), F32),
        ],
        compiler_params=_cparams(("parallel",)),
        name="inproj",
    )(x2, g, w_qkv, w_f, cos_t, sp_t, sm_t)


def _cumsum_rows(x, tri, exact_small_ints):
    n = x.shape[0]
    carry = jnp.zeros((1, LANE), F32)
    out = []
    for b in range(n // LANE):
        xb = x[b * LANE:(b + 1) * LANE]
        if exact_small_ints:
            y = _dot(tri, xb.astype(BF16))
        else:
            hi, mid, lo = _split3(xb)
            y = _dot(tri, hi) + _dot(tri, mid) + _dot(tri, lo)
        y = y + carry
        carry = y[LANE - 1:LANE, :]
        out.append(y)
    return out


def _forget_kernel(f_ref, bf_ref, tri_ref, ccol_ref, crow_ref):
    z = f_ref[0] + bf_ref[...]
    logf = jnp.minimum(z, 0.0) - jnp.log(1.0 + jnp.exp(-jnp.abs(z)))
    blocks = _cumsum_rows(logf, tri_ref[...], exact_small_ints=False)
    for b, y in enumerate(blocks):
        ccol_ref[0, b * LANE:(b + 1) * LANE, :] = y
        crow_ref[0, :, b * LANE:(b + 1) * LANE] = y.T[:8, :]


def _forget_cumsum(f3, bf_pad, tri):
    B = f3.shape[0]
    return pl.pallas_call(
        _forget_kernel,
        grid=(B,),
        in_specs=[
            pl.BlockSpec((1, SEQ, LANE), lambda b: (b, 0, 0)),
            pl.BlockSpec((1, LANE), lambda b: (0, 0)),
            pl.BlockSpec((LANE, LANE), lambda b: (0, 0)),
        ],
        out_specs=[
            pl.BlockSpec((1, SEQ, LANE), lambda b: (b, 0, 0)),
            pl.BlockSpec((1, 8, SEQ), lambda b: (b, 0, 0)),
        ],
        out_shape=[
            jax.ShapeDtypeStruct((B, SEQ, LANE), F32),
            jax.ShapeDtypeStruct((B, 8, SEQ), F32),
        ],
        compiler_params=_cparams(("parallel",)),
        name="forget_cumsum",
    )(f3, bf_pad, tri)


def _online_step(state, s, v_blk):
    m, l, acc = state
    m_new = jnp.maximum(m, jnp.max(s, axis=-1, keepdims=True))
    p = jnp.exp(s - m_new)
    alpha = jnp.exp(m - m_new)
    l = alpha * l + jnp.sum(p, axis=-1, keepdims=True)
    acc = alpha * acc + _dot(p.astype(BF16), v_blk)
    return m_new, l, acc


def _attn_kernel(mode, h0, q_ref, k_ref, v_ref, *rest):
    if mode == "fox":
        ccol_ref, crow_ref, o_ref = rest
    elif mode == "moba":
        o_ref, km_scr, sel_scr = rest
    else:
        lm_ref, o_ref = rest
    j = pl.program_id(1)
    h = h0 + j
    off = (h % 2) * HEAD_DIM
    lane = lax.broadcasted_iota(jnp.int32, (1, LANE), 1)
    hmask = (lane >= off) & (lane < off + HEAD_DIM)
    first_visit = (j == 0) | (h % 2 == 0)
    scale = HEAD_DIM ** -0.5

    rows = lax.broadcasted_iota(jnp.int32, (ATT_BLK, ATT_BLK), 0)
    cols = lax.broadcasted_iota(jnp.int32, (ATT_BLK, ATT_BLK), 1)
    causal = rows >= cols

    if mode == "moba":
        km_scr[...] = jnp.zeros((LANE, LANE), F32)
        for n in range(N_ATT_BLK):
            kb = k_ref[0, n * ATT_BLK:(n + 1) * ATT_BLK, :].astype(F32)
            km = jnp.sum(kb, axis=0, keepdims=True) * (1.0 / MOBA_BLOCK)
            km_scr[n:n + 1, :] = jnp.where(hmask, km, 0.0)
        km_hi, km_mid, km_lo = _split3(km_scr[...])
        sub8 = lax.broadcasted_iota(jnp.int32, (8, ATT_BLK), 0)

    for i in range(N_ATT_BLK):
        r0 = i * ATT_BLK
        q_blk = q_ref[0, r0:r0 + ATT_BLK, :]
        qm = jnp.where(hmask, q_blk * scale, 0.0).astype(BF16)

        if mode == "fox":
            ct = jnp.sum(jnp.where(lane == h, ccol_ref[0, r0:r0 + ATT_BLK, :], 0.0),
                         axis=1, keepdims=True)
        if mode == "moba" and i > MOBA_TOPK:
            gt = (_dot_nt(km_hi, qm) + _dot_nt(km_mid, qm) + _dot_nt(km_lo, qm))[:8, :]
            rank = jnp.zeros((8, ATT_BLK), F32)
            for n2 in range(i):
                g2 = gt[n2:n2 + 1, :]
                ahead = (g2 > gt) | ((g2 == gt) & (n2 < sub8))
                rank = rank + jnp.where(ahead, 1.0, 0.0)
            sel_t = jnp.where((rank < MOBA_TOPK) & (sub8 < i), 1.0, 0.0)
            sel_scr[...] = jnp.zeros((LANE, ATT_BLK), F32)
            sel_scr[0:8, :] = sel_t
            sel = sel_scr[...].T

        k_blk = k_ref[0, r0:r0 + ATT_BLK, :]
        v_blk = v_ref[0, r0:r0 + ATT_BLK, :]
        s = _dot_nt(qm, k_blk)
        if mode == "fox":
            s = s + (ct - crow_ref[0, pl.ds(h, 1), r0:r0 + ATT_BLK])
            s = jnp.where(causal, s, NEG)
        elif mode == "moba":
            s = jnp.where(causal, s, NEG)
        else:
            s = s + lm_ref[0]
        state = (jnp.full((ATT_BLK, 1), NEG, F32), jnp.zeros((ATT_BLK, 1), F32),
                 jnp.zeros((ATT_BLK, LANE), F32))
        state = _online_step(state, s, v_blk)

        for n in range(i):
            c0 = n * ATT_BLK
            k_blk = k_ref[0, c0:c0 + ATT_BLK, :]
            v_blk = v_ref[0, c0:c0 + ATT_BLK, :]
            s = _dot_nt(qm, k_blk)
            if mode == "fox":
                s = s + (ct - crow_ref[0, pl.ds(h, 1), c0:c0 + ATT_BLK])
            elif mode == "moba":
                if i > MOBA_TOPK:
                    s = jnp.where(sel[:, n:n + 1] > 0.5, s, NEG)
            else:
                s = s + lm_ref[i - n]
            state = _online_step(state, s, v_blk)

        _, l, acc = state
        o = jnp.where(hmask, acc * (1.0 / l), 0.0)

        @pl.when(first_visit)
        def _():
            o_ref[0, r0:r0 + ATT_BLK, :] = o

        @pl.when(jnp.logical_not(first_visit))
        def _():
            o_ref[0, r0:r0 + ATT_BLK, :] = o_ref[0, r0:r0 + ATT_BLK, :] + o


def _attention(mode, h0, n_heads, qkv3, extras):
    B = qkv3.shape[0]
    n_pairs = N_HEADS // 2
    pair0 = h0 // 2
    n_out_pairs = (h0 + n_heads - 1) // 2 - pair0 + 1

    def col(part):
        return lambda b, j: (b, 0, part * n_pairs + (h0 + j) // 2)

    in_specs = [pl.BlockSpec((1, SEQ, LANE), col(p)) for p in range(3)]
    scratch = []
    if mode == "fox":
        in_specs += [pl.BlockSpec((1, SEQ, LANE), lambda b, j: (b, 0, 0)),
                     pl.BlockSpec((1, 8, SEQ), lambda b, j: (b, 0, 0))]
    elif mode == "moba":
        scratch = [pltpu.VMEM((LANE, LANE), F32), pltpu.VMEM((LANE, ATT_BLK), F32)]
    else:
        in_specs += [pl.BlockSpec((N_ATT_BLK, ATT_BLK, ATT_BLK), lambda b, j: (0, 0, 0))]
    return pl.pallas_call(
        functools.partial(_attn_kernel, mode, h0),
        grid=(B, n_heads),
        in_specs=in_specs,
        out_specs=pl.BlockSpec((1, SEQ, LANE), lambda b, j: (b, 0, (h0 + j) // 2 - pair0)),
        out_shape=jax.ShapeDtypeStruct((B, SEQ, n_out_pairs * LANE), F32),
        scratch_shapes=scratch,
        compiler_params=_cparams(("parallel", "arbitrary")),
        name="attn_" + mode,
    )(qkv3, qkv3, qkv3, *extras)


def _dilated_log_multiplicity():
    r = np.arange(ATT_BLK)[:, None]
    c = np.arange(ATT_BLK)[None, :]
    out = np.empty((N_ATT_BLK, ATT_BLK, ATT_BLK), np.float32)
    for d in range(N_ATT_BLK):
        dist = d * ATT_BLK + r - c
        mult = np.zeros_like(dist)
        for window, dil in DIL_PATTERNS:
            mult += ((dist >= 0) & (dist % dil == 0) & (dist // dil <= window // dil)).astype(dist.dtype)
        out[d] = np.where(mult > 0, np.log(np.maximum(mult, 1)), NEG)
    return out


OUT_TM = 512
MIX_W = 3 * LANE


def _outproj_kernel(x_ref, oa_ref, ob_ref, oc_ref, ga_ref, gb_ref, gc_ref, wa_ref, wb_ref, wc_ref, o_ref):
    def normed(o_r, g_r, width):
        o = o_r[...]
        ms = jnp.sum(o * o, axis=-1, keepdims=True) * (1.0 / width)
        return (o * lax.rsqrt(ms + EPS) * g_r[...]).astype(BF16)

    ya = normed(oa_ref, ga_ref, H_FOX * HEAD_DIM)
    yb = normed(ob_ref, gb_ref, H_MOBA * HEAD_DIM)
    yc = normed(oc_ref, gc_ref, H_DIL * HEAD_DIM)
    o_ref[...] = x_ref[...] + _dot(ya, wa_ref[...]) + _dot(yb, wb_ref[...]) + _dot(yc, wc_ref[...])


def _outproj(x2, oa, ob, oc, ga, gb, gc, wa, wb, wc):
    T = x2.shape[0]
    row = lambda w: pl.BlockSpec((OUT_TM, w), lambda i: (i, 0))
    full = lambda a, b: pl.BlockSpec((a, b), lambda i: (0, 0))
    return pl.pallas_call(
        _outproj_kernel,
        grid=(T // OUT_TM,),
        in_specs=[row(D_MODEL), row(MIX_W), row(MIX_W), row(MIX_W),
                  full(1, MIX_W), full(1, MIX_W), full(1, MIX_W),
                  full(MIX_W, D_MODEL), full(MIX_W, D_MODEL), full(MIX_W, D_MODEL)],
        out_specs=row(D_MODEL),
        out_shape=jax.ShapeDtypeStruct((T, D_MODEL), F32),
        compiler_params=_cparams(("parallel",)),
        name="outproj",
    )(x2, oa, ob, oc, ga, gb, gc, wa, wb, wc)


FFN_TM = 1024
FFN_TF = 512


def _swiglu_act(a, b):
    return a * (1.0 / (1.0 + jnp.exp(-a))) * b


def _ffn_kernel(x_ref, g_ref, w1_ref, w3_ref, w2_ref, o_ref, h_scr, acc_scr):
    f = pl.program_id(1)

    @pl.when(f == 0)
    def _():
        h_scr[...] = _rms(x_ref[...], g_ref[...]).astype(BF16)

    h = h_scr[...]
    act = _swiglu_act(_dot(h, w1_ref[...]), _dot(h, w3_ref[...])).astype(BF16)
    y = _dot(act, w2_ref[...])

    @pl.when(f == 0)
    def _():
        acc_scr[...] = y

    @pl.when(f > 0)
    def _():
        acc_scr[...] = acc_scr[...] + y

    @pl.when(f == pl.num_programs(1) - 1)
    def _():
        o_ref[...] = x_ref[...] + acc_scr[...]


def _ffn(x2, g, w1, w3, w2):
    T = x2.shape[0]
    return pl.pallas_call(
        _ffn_kernel,
        grid=(T // FFN_TM, D_FF // FFN_TF),
        in_specs=[
            pl.BlockSpec((FFN_TM, D_MODEL), lambda i, f: (i, 0)),
            pl.BlockSpec((1, D_MODEL), lambda i, f: (0, 0)),
            pl.BlockSpec((D_MODEL, FFN_TF), lambda i, f: (0, f)),
            pl.BlockSpec((D_MODEL, FFN_TF), lambda i, f: (0, f)),
            pl.BlockSpec((FFN_TF, D_MODEL), lambda i, f: (f, 0)),
        ],
        out_specs=pl.BlockSpec((FFN_TM, D_MODEL), lambda i, f: (i, 0)),
        out_shape=jax.ShapeDtypeStruct((T, D_MODEL), F32),
        scratch_shapes=[pltpu.VMEM((FFN_TM, D_MODEL), BF16), pltpu.VMEM((FFN_TM, D_MODEL), F32)],
        compiler_params=_cparams(("parallel", "arbitrary")),
        name="ffn_dense",
    )(x2, g, w1, w3, w2)


MOE_TT = 2048
MOE_CH = 256
MOE_UNIT = 128
MOE_TF = 512
MOE_KB = 256
MOE_NKB = MOE_TT // MOE_KB


def _router_kernel(x_ref, g_ref, wr_hi_ref, wr_lo_ref, tri_ref,
                   h_ref, posc_ref, gatec_ref, posr_ref, cnt_ref):
    h = _rms(x_ref[...], g_ref[...])
    h_hi = h.astype(BF16)
    h_lo = (h - h_hi.astype(F32)).astype(BF16)
    h_ref[...] = h_hi
    wr_hi, wr_lo = wr_hi_ref[...], wr_lo_ref[...]
    logits = _dot(h_hi, wr_hi) + _dot(h_lo, wr_hi) + _dot(h_hi, wr_lo)
    lane = lax.broadcasted_iota(jnp.int32, (1, LANE), 1).astype(F32)
    lg = jnp.where(lane < N_EXPERTS, logits, -jnp.inf)
    m1 = jnp.max(lg, axis=1, keepdims=True)
    i1 = jnp.min(jnp.where(lg == m1, lane, float(LANE)), axis=1, keepdims=True)
    lg2 = jnp.where(lane == i1, -jnp.inf, lg)
    m2 = jnp.max(lg2, axis=1, keepdims=True)
    i2 = jnp.min(jnp.where(lg2 == m2, lane, float(LANE)), axis=1, keepdims=True)
    e2 = jnp.exp(m2 - m1)
    g1 = 1.0 / (1.0 + e2)
    g2 = e2 * g1
    routed = (lane == i1) | (lane == i2)
    gatec_ref[...] = jnp.where(lane == i1, g1, jnp.where(lane == i2, g2, 0.0))
    routed_f = jnp.where(routed, 1.0, 0.0)
    blocks = _cumsum_rows(routed_f, tri_ref[...], exact_small_ints=True)
    for b, cnt in enumerate(blocks):
        sl = slice(b * LANE, (b + 1) * LANE)
        pos = jnp.where(routed_f[sl] > 0.5, cnt - 1.0, -1.0)
        posc_ref[sl, :] = pos
        posr_ref[0, :, sl] = pos.T[:8, :]
    per_kb = MOE_KB // LANE
    for kb in range(MOE_NKB):
        cnt_ref[0, kb:kb + 1, :] = blocks[(kb + 1) * per_kb - 1][LANE - 1:LANE, :]


def _router(x2, g, wr_hi, wr_lo, tri):
    T = x2.shape[0]
    nt = T // MOE_TT
    return pl.pallas_call(
        _router_kernel,
        grid=(nt,),
        in_specs=[
            pl.BlockSpec((MOE_TT, D_MODEL), lambda i: (i, 0)),
            pl.BlockSpec((1, D_MODEL), lambda i: (0, 0)),
            pl.BlockSpec((D_MODEL, LANE), lambda i: (0, 0)),
            pl.BlockSpec((D_MODEL, LANE), lambda i: (0, 0)),
            pl.BlockSpec((LANE, LANE), lambda i: (0, 0)),
        ],
        out_specs=[
            pl.BlockSpec((MOE_TT, D_MODEL), lambda i: (i, 0)),
            pl.BlockSpec((MOE_TT, LANE), lambda i: (i, 0)),
            pl.BlockSpec((MOE_TT, LANE), lambda i: (i, 0)),
            pl.BlockSpec((1, 8, MOE_TT), lambda i: (i, 0, 0)),
            pl.BlockSpec((1, MOE_NKB, LANE), lambda i: (i, 0, 0)),
        ],
        out_shape=[
            jax.ShapeDtypeStruct((T, D_MODEL), BF16),
            jax.ShapeDtypeStruct((T, LANE), F32),
            jax.ShapeDtypeStruct((T, LANE), F32),
            jax.ShapeDtypeStruct((nt, 8, MOE_TT), F32),
            jax.ShapeDtypeStruct((nt, MOE_NKB, LANE), F32),
        ],
        compiler_params=_cparams(("parallel",)),
        name="moe_router",
    )(x2, g, wr_hi, wr_lo, tri)


def _moe_kernel(pref_ref, h_ref, posc_ref, gatec_ref, posr_ref, w1_ref, w3_ref, w2_ref, o_ref,
                xe_scr, ye_scr, pe_scr, ge_scr, acc_scr):
    t, e, f = pl.program_id(0), pl.program_id(1), pl.program_id(2)
    nf = pl.num_programs(2)
    base = (t * N_EXPERTS + e) * (MOE_NKB + 1)
    pref = [pref_ref[base + kb] for kb in range(MOE_NKB + 1)]
    n_rows = pref[MOE_NKB]
    lane = lax.broadcasted_iota(jnp.int32, (1, LANE), 1)

    @pl.when((e == 0) & (f == 0))
    def _():
        o_ref[...] = jnp.zeros((MOE_TT, D_MODEL), F32)

    @pl.when(f == 0)
    def _():
        is_e = lane == e
        pe_scr[...] = jnp.sum(jnp.where(is_e, posc_ref[...], 0.0), axis=1, keepdims=True)
        ge_scr[...] = jnp.sum(jnp.where(is_e, gatec_ref[...], 0.0), axis=1, keepdims=True)
        pos_row = posr_ref[0, pl.ds(e, 1), :]
        slot = lax.broadcasted_iota(jnp.int32, (MOE_CH, 1), 0).astype(F32)

        def gather(c, _):
            r0 = pl.multiple_of(c * MOE_CH, MOE_CH)
            rows = pl.ds(r0, MOE_CH)
            slots = slot + r0.astype(F32)
            acc_scr[...] = jnp.zeros((MOE_CH, D_MODEL), F32)
            ye_scr[rows, :] = jnp.zeros((MOE_CH, D_MODEL), F32)
            for kb in range(MOE_NKB):
                @pl.when((pref[kb] < r0 + MOE_CH) & (pref[kb + 1] > r0))
                def _():
                    tok = slice(kb * MOE_KB, (kb + 1) * MOE_KB)
                    onehot = jnp.where(pos_row[:, tok] == slots, 1.0, 0.0).astype(BF16)
                    acc_scr[...] = acc_scr[...] + _dot(onehot, h_ref[tok, :])
            xe_scr[rows, :] = acc_scr[...].astype(BF16)
            return 0

        lax.fori_loop(0, (n_rows + MOE_CH - 1) // MOE_CH, gather, 0)

    def expert_rows(r0, n):
        rows = pl.ds(r0, n)
        xc = xe_scr[rows, :]
        act = _swiglu_act(_dot(xc, w1_ref[0]), _dot(xc, w3_ref[0])).astype(BF16)
        ye_scr[rows, :] = ye_scr[rows, :] + _dot(act, w2_ref[0])

    n_units = (n_rows + MOE_UNIT - 1) // MOE_UNIT
    units_per_trip = 2 * MOE_CH // MOE_UNIT

    def two_chunks(i, _):
        r0 = pl.multiple_of(i * (2 * MOE_CH), 2 * MOE_CH)
        expert_rows(r0, MOE_CH)
        expert_rows(r0 + MOE_CH, MOE_CH)
        return 0

    n_trips = n_units // units_per_trip
    lax.fori_loop(0, n_trips, two_chunks, 0)
    left = n_units - n_trips * units_per_trip
    tail0 = pl.multiple_of(n_trips * (2 * MOE_CH), 2 * MOE_CH)

    @pl.when(left >= 2)
    def _():
        expert_rows(tail0, MOE_CH)

    @pl.when(left == 1)
    def _():
        expert_rows(tail0, MOE_UNIT)

    @pl.when(left == 3)
    def _():
        expert_rows(tail0 + MOE_CH, MOE_UNIT)

    @pl.when(f == nf - 1)
    def _():
        slot_l = lax.broadcasted_iota(jnp.int32, (1, MOE_CH), 1).astype(F32)
        for kb in range(MOE_NKB):
            lo, hi = pref[kb], pref[kb + 1]

            @pl.when(hi > lo)
            def _():
                tok = slice(kb * MOE_KB, (kb + 1) * MOE_KB)
                pe_b = pe_scr[tok, :]
                ge_b = ge_scr[tok, :]

                def combine(c, _):
                    r0 = pl.multiple_of(c * MOE_CH, MOE_CH)
                    onehot_t = jnp.where(pe_b == slot_l + r0.astype(F32), 1.0, 0.0).astype(BF16)
                    y = ye_scr[pl.ds(r0, MOE_CH), :].astype(BF16)
                    o_ref[tok, :] = o_ref[tok, :] + ge_b * _dot(onehot_t, y)
                    return 0

                lax.fori_loop(lo // MOE_CH, (hi - 1) // MOE_CH + 1, combine, 0)


def _moe(counts, h, posc, gatec, posr, we1, we3, we2):
    T = h.shape[0]
    nt = T // MOE_TT
    grid_spec = pltpu.PrefetchScalarGridSpec(
        num_scalar_prefetch=1,
        grid=(nt, N_EXPERTS, D_FF // MOE_TF),
        in_specs=[
            pl.BlockSpec((MOE_TT, D_MODEL), lambda t, e, f, c: (t, 0)),
            pl.BlockSpec((MOE_TT, LANE), lambda t, e, f, c: (t, 0)),
            pl.BlockSpec((MOE_TT, LANE), lambda t, e, f, c: (t, 0)),
            pl.BlockSpec((1, 8, MOE_TT), lambda t, e, f, c: (t, 0, 0)),
            pl.BlockSpec((1, D_MODEL, MOE_TF), lambda t, e, f, c: (e, 0, f)),
            pl.BlockSpec((1, D_MODEL, MOE_TF), lambda t, e, f, c: (e, 0, f)),
            pl.BlockSpec((1, MOE_TF, D_MODEL), lambda t, e, f, c: (e, f, 0)),
        ],
        out_specs=pl.BlockSpec((MOE_TT, D_MODEL), lambda t, e, f, c: (t, 0)),
        scratch_shapes=[
            pltpu.VMEM((MOE_TT, D_MODEL), BF16),
            pltpu.VMEM((MOE_TT, D_MODEL), F32),
            pltpu.VMEM((MOE_TT, 1), F32),
            pltpu.VMEM((MOE_TT, 1), F32),
            pltpu.VMEM((MOE_CH, D_MODEL), F32),
        ],
    )
    return pl.pallas_call(
        _moe_kernel,
        grid_spec=grid_spec,
        out_shape=jax.ShapeDtypeStruct((T, D_MODEL), F32),
        compiler_params=_cparams(("parallel", "arbitrary", "arbitrary")),
        name="moe_experts",
    )(counts, h, posc, gatec, posr, we1, we3, we2)


FIN_TM = 1024


def _final_kernel(x_ref, y_ref, g_ref, o_ref):
    o_ref[...] = _rms(x_ref[...] + y_ref[...], g_ref[...])


def _final(x2, y2, g):
    T = x2.shape[0]
    row = pl.BlockSpec((FIN_TM, D_MODEL), lambda i: (i, 0))
    return pl.pallas_call(
        _final_kernel,
        grid=(T // FIN_TM,),
        in_specs=[row, row, pl.BlockSpec((1, D_MODEL), lambda i: (0, 0))],
        out_specs=row,
        out_shape=jax.ShapeDtypeStruct((T, D_MODEL), F32),
        compiler_params=_cparams(("parallel",)),
        name="final_norm",
    )(x2, y2, g)


def _rope_lane_tables():
    half = ROT_DIM // 2
    inv = ROPE_THETA ** (-jnp.arange(0, ROT_DIM, 2, dtype=F32) / ROT_DIM)
    ang = jnp.arange(SEQ, dtype=F32)[:, None] * inv[None, :]
    cos, sin = jnp.cos(ang), jnp.sin(ang)
    ones = jnp.ones((SEQ, HEAD_DIM - ROT_DIM), F32)
    zeros = jnp.zeros((SEQ, HEAD_DIM - ROT_DIM), F32)
    z8 = jnp.zeros((SEQ, half), F32)
    cos_h = jnp.concatenate([cos, cos, ones], axis=1)
    sp_h = jnp.concatenate([z8, sin, zeros], axis=1)
    sm_h = jnp.concatenate([-sin, z8, zeros], axis=1)
    two = lambda a: jnp.concatenate([a, a], axis=1)
    return two(cos_h), two(sp_h), two(sm_h)


def _pad_lanes(a, width=LANE):
    return jnp.pad(a, ((0, 0), (0, width - a.shape[1])))


def _mixer(x2, B, norm_mix, w_in, b_f, mix_gain, w_out, tables, tri, lm):
    w_qkv = w_in[:, :QKV_W].astype(BF16)
    w_f = _pad_lanes(w_in[:, QKV_W:]).astype(BF16)
    qkv, f_logit = _inproj(x2, norm_mix[None, :], w_qkv, w_f, *tables)
    ccol, crow = _forget_cumsum(f_logit.reshape(B, SEQ, LANE), _pad_lanes(b_f[None, :]), tri)
    qkv3 = qkv.reshape(B, SEQ, QKV_W)
    oa = _attention("fox", 0, H_FOX, qkv3, (ccol, crow))
    ob = _attention("moba", H_FOX, H_MOBA, qkv3, ())
    oc = _attention("dil", H_FOX + H_MOBA, H_DIL, qkv3, (lm,))
    a0, b0, c0 = 0, (H_FOX // 2) * PAIR, ((H_FOX + H_MOBA) // 2) * PAIR
    T = x2.shape[0]
    sl = lambda v, s: v[s:s + MIX_W]
    return _outproj(
        x2, oa.reshape(T, MIX_W), ob.reshape(T, MIX_W), oc.reshape(T, MIX_W),
        sl(mix_gain, a0)[None, :], sl(mix_gain, b0)[None, :], sl(mix_gain, c0)[None, :],
        sl(w_out, a0).astype(BF16), sl(w_out, b0).astype(BF16), sl(w_out, c0).astype(BF16))


def kernel(x, l0_norm_mix, l0_w_in, l0_b_f, l0_mix_gain, l0_w_out, l0_norm_ffn, l0_w1, l0_w3, l0_w2, l1_norm_mix, l1_w_in, l1_b_f, l1_mix_gain, l1_w_out, l1_norm_ffn, l1_w_router, l1_we1, l1_we3, l1_we2, final_norm):
    B, S, D = x.shape
    assert (S, D) == (SEQ, D_MODEL)
    T = B * S
    tables = _rope_lane_tables()
    tri = jnp.asarray(np.tril(np.ones((LANE, LANE), np.float32)), BF16)
    lm = jnp.asarray(_dilated_log_multiplicity())

    x2 = x.reshape(T, D)
    x2 = _mixer(x2, B, l0_norm_mix, l0_w_in, l0_b_f, l0_mix_gain, l0_w_out, tables, tri, lm)
    x2 = _ffn(x2, l0_norm_ffn[None, :], l0_w1.astype(BF16), l0_w3.astype(BF16), l0_w2.astype(BF16))
    x2 = _mixer(x2, B, l1_norm_mix, l1_w_in, l1_b_f, l1_mix_gain, l1_w_out, tables, tri, lm)

    wr = _pad_lanes(l1_w_router)
    wr_hi = wr.astype(BF16)
    wr_lo = (wr - wr_hi.astype(F32)).astype(BF16)
    h, posc, gatec, posr, cnt = _router(x2, l1_norm_ffn[None, :], wr_hi, wr_lo, tri)
    ends = cnt[:, :, :N_EXPERTS].astype(jnp.int32).transpose(0, 2, 1)
    pref = jnp.pad(ends, ((0, 0), (0, 0), (1, 0))).reshape(-1)
    y2 = _moe(pref, h, posc, gatec, posr,
              l1_we1.astype(BF16), l1_we3.astype(BF16), l1_we2.astype(BF16))
    return _final(x2, y2, final_norm[None, :]).reshape(B, S, D)
```

```python
import functools
import math

import numpy as np
import jax
import jax.numpy as jnp
from jax import lax
from jax.experimental import pallas as pl
from jax.experimental.pallas import tpu as pltpu

D_MODEL = 1024
SEQ = 2048
HEAD_DIM = 64
N_HEADS = 16
H_FOX = 5
H_MOBA = 5
H_DIL = 6
ROT_DIM = 16
ROPE_THETA = 500000.0
MOBA_BLOCK = 256
MOBA_TOPK = 3
DIL_PATTERNS = ((128, 1), (512, 4), (2048, 16))
D_FF = 3584
N_EXPERTS = 8
EPS = 1e-6

LANE = 128
PAIR = 2 * HEAD_DIM
MIX = N_HEADS * HEAD_DIM
FOX_SLOTS = 2 * ((H_FOX + 1) // 2)
QK_W = 2 * MIX
QKV_W = 3 * MIX
ATT_BLK = 256
N_ATT_BLK = SEQ // ATT_BLK
NEG = -1e30
LOG2E = math.log2(math.e)
Q_SCALE = HEAD_DIM ** -0.5 * LOG2E
VMEM_LIMIT = 56 * 1024 * 1024

BF16 = jnp.bfloat16
F32 = jnp.float32


def _dot(a, b):
    return jnp.dot(a, b, preferred_element_type=F32)


def _dot_nt(a, b):
    return lax.dot_general(a, b, (((1,), (1,)), ((), ())), preferred_element_type=F32)


def _split3(x):
    hi = x.astype(BF16)
    r1 = x - hi.astype(F32)
    mid = r1.astype(BF16)
    lo = (r1 - mid.astype(F32)).astype(BF16)
    return hi, mid, lo


def _rms(x, g):
    return x * lax.rsqrt(jnp.mean(x * x, axis=-1, keepdims=True) + EPS) * g


def _cparams(sem):
    return pltpu.CompilerParams(dimension_semantics=sem, vmem_limit_bytes=VMEM_LIMIT)


IN_TM = 512
IN_TN = 512


def _inproj_kernel(x_ref, g_ref, w_ref, wvt_ref, wf_ref, cos_ref, sp_ref, sm_ref, qk_ref, vt_ref, f_ref):
    hb = _rms(x_ref[...], g_ref[...]).astype(BF16)
    f_ref[...] = _dot(hb, wf_ref[...])
    vt_ref[0] = _dot_nt(wvt_ref[...], hb).astype(BF16)
    cos, sp, sm = cos_ref[...], sp_ref[...], sm_ref[...]
    lane = lax.broadcasted_iota(jnp.int32, (1, LANE), 1)
    upper = lane >= HEAD_DIM
    n_pairs = N_HEADS // 2
    first_rot_head = H_FOX
    for c in range(QK_W // IN_TN):
        y = _dot(hb, w_ref[:, c * IN_TN:(c + 1) * IN_TN])
        for t in range(IN_TN // LANE):
            col = c * (IN_TN // LANE) + t
            part, pair = divmod(col, n_pairs)
            yt = y[:, t * LANE:(t + 1) * LANE]
            lo_head, hi_head = 2 * pair, 2 * pair + 1
            if hi_head >= first_rot_head:
                c_t, sp_t, sm_t = cos, sp, sm
                if lo_head < first_rot_head:
                    c_t = jnp.where(upper, cos, 1.0)
                    sp_t = jnp.where(upper, sp, 0.0)
                    sm_t = jnp.where(upper, sm, 0.0)
                half = ROT_DIM // 2
                yt = (yt * c_t + pltpu.roll(yt, half, axis=1) * sp_t
                      + pltpu.roll(yt, LANE - half, axis=1) * sm_t)
            if part == 0:
                yt = yt * Q_SCALE
            qk_ref[:, col * LANE:(col + 1) * LANE] = yt.astype(BF16)


def _inproj(x2, g, w_qk, w_vt, w_f, cos_t, sp_t, sm_t):
    T = x2.shape[0]
    n_pos = SEQ // IN_TM
    tab = pl.BlockSpec((IN_TM, LANE), lambda i: (i % n_pos, 0))
    return pl.pallas_call(
        _inproj_kernel,
        grid=(T // IN_TM,),
        in_specs=[
            pl.BlockSpec((IN_TM, D_MODEL), lambda i: (i, 0)),
            pl.BlockSpec((1, D_MODEL), lambda i: (0, 0)),
            pl.BlockSpec((D_MODEL, QK_W), lambda i: (0, 0)),
            pl.BlockSpec((MIX, D_MODEL), lambda i: (0, 0)),
            pl.BlockSpec((D_MODEL, LANE), lambda i: (0, 0)),
            tab, tab, tab,
        ],
        out_specs=[
            pl.BlockSpec((IN_TM, QK_W), lambda i: (i, 0)),
            pl.BlockSpec((1, MIX, IN_TM), lambda i: (i // n_pos, 0, i % n_pos)),
            pl.BlockSpec((IN_TM, LANE), lambda i: (i, 0)),
        ],
        out_shape=[
            jax.ShapeDtypeStruct((T, QK_W), BF16),
            jax.ShapeDtypeStruct((T // SEQ, MIX, SEQ), BF16),
            jax.ShapeDtypeStruct((T, LANE), F32),
        ],
        compiler_params=_cparams(("parallel",)),
        name="inproj",
    )(x2, g, w_qk, w_vt, w_f, cos_t, sp_t, sm_t)


def _cumsum_rows(x, tri, exact_small_ints):
    n = x.shape[0]
    carry = jnp.zeros((1, LANE), F32)
    out = []
    for b in range(n // LANE):
        xb = x[b * LANE:(b + 1) * LANE]
        if exact_small_ints:
            y = _dot(tri, xb.astype(BF16))
        else:
            hi, mid, lo = _split3(xb)
            y = _dot(tri, hi) + _dot(tri, mid) + _dot(tri, lo)
        y = y + carry
        carry = y[LANE - 1:LANE, :]
        out.append(y)
    return out


def _forget_kernel(f_ref, bf_ref, tri_ref, cbc_ref, crow_ref):
    z = f_ref[0] + bf_ref[...]
    logf = (jnp.minimum(z, 0.0) - jnp.log(1.0 + jnp.exp(-jnp.abs(z)))) * LOG2E
    blocks = _cumsum_rows(logf, tri_ref[...], exact_small_ints=False)
    for b, y in enumerate(blocks):
        yt = y.T
        for h in range(FOX_SLOTS):
            live = h < H_FOX
            cbc_ref[0, h, b * LANE:(b + 1) * LANE, :] = (
                jnp.broadcast_to(y[:, h:h + 1], (LANE, LANE)) if live else jnp.zeros((LANE, LANE), F32))
            crow_ref[0, h, :, b * LANE:(b + 1) * LANE] = yt[h:h + 1, :] if live else jnp.zeros((1, LANE), F32)


def _forget_cumsum(f3, bf_pad, tri):
    B = f3.shape[0]
    return pl.pallas_call(
        _forget_kernel,
        grid=(B,),
        in_specs=[
            pl.BlockSpec((1, SEQ, LANE), lambda b: (b, 0, 0)),
            pl.BlockSpec((1, LANE), lambda b: (0, 0)),
            pl.BlockSpec((LANE, LANE), lambda b: (0, 0)),
        ],
        out_specs=[
            pl.BlockSpec((1, FOX_SLOTS, SEQ, LANE), lambda b: (b, 0, 0, 0)),
            pl.BlockSpec((1, FOX_SLOTS, 1, SEQ), lambda b: (b, 0, 0, 0)),
        ],
        out_shape=[
            jax.ShapeDtypeStruct((B, FOX_SLOTS, SEQ, LANE), F32),
            jax.ShapeDtypeStruct((B, FOX_SLOTS, 1, SEQ), F32),
        ],
        compiler_params=_cparams(("parallel",)),
        name="forget_cumsum",
    )(f3, bf_pad, tri)


def _fold8(x, op):
    parts = [x[r:r + 8] for r in range(0, x.shape[0], 8)]
    while len(parts) > 1:
        parts = [op(parts[a], parts[a + 1]) for a in range(0, len(parts) - 1, 2)] + parts[len(parts) & ~1:]
    return parts[0]


def _attn_kernel(mode, h_first, h_last, q_ref, k_ref, vt_ref, *rest):
    if mode == "fox":
        cbc_ref, crow_ref, o_ref, s_scr = rest
    elif mode == "moba":
        o_ref, s_scr, km_scr = rest
    else:
        lmt_ref, o_ref, s_scr = rest
    pair = h_first // 2 + pl.program_id(1)
    lane = lax.broadcasted_iota(jnp.int32, (1, LANE), 1)
    key_idx = lax.broadcasted_iota(jnp.int32, (ATT_BLK, ATT_BLK), 0)
    qry_idx = lax.broadcasted_iota(jnp.int32, (ATT_BLK, ATT_BLK), 1)
    causal = key_idx <= qry_idx
    sub8 = lax.broadcasted_iota(jnp.int32, (8, ATT_BLK), 0)

    def block_means(slot, hmask):
        km_scr[slot] = jnp.zeros((LANE, LANE), F32)
        for n in range(N_ATT_BLK):
            kb = k_ref[0, n * ATT_BLK:(n + 1) * ATT_BLK, :].astype(F32)
            km = jnp.sum(kb, axis=0, keepdims=True) * (1.0 / MOBA_BLOCK)
            km_scr[slot, n:n + 1, :] = jnp.where(hmask, km, 0.0)
        return _split3(km_scr[slot])

    def query_block(slot, i, hmask, km3):
        r0 = i * ATT_BLK
        qm = jnp.where(hmask, q_ref[0, r0:r0 + ATT_BLK, :], 0.0).astype(BF16)
        if mode == "moba" and i > MOBA_TOPK:
            gt = (_dot_nt(km3[0], qm) + _dot_nt(km3[1], qm) + _dot_nt(km3[2], qm))[:8, :]
            rank = jnp.zeros((8, ATT_BLK), F32)
            for n2 in range(i):
                g2 = gt[n2:n2 + 1, :]
                ahead = (g2 > gt) | ((g2 == gt) & (n2 < sub8))
                rank = rank + jnp.where(ahead, 1.0, 0.0)
            sel_t = jnp.where((rank < MOBA_TOPK) & (sub8 < i), 1.0, 0.0)

        m8 = jnp.full((8, ATT_BLK), NEG, F32)
        for n in range(i + 1):
            c0 = n * ATT_BLK
            st = _dot_nt(k_ref[0, c0:c0 + ATT_BLK, :], qm)
            if mode == "fox":
                cs = cbc_ref[0, slot, c0:c0 + ATT_BLK, :]
                st = st - jnp.concatenate([cs, cs], axis=1)
            elif mode == "dil":
                st = st + lmt_ref[i - n]
            elif n < i and i > MOBA_TOPK:
                st = jnp.where(sel_t[n:n + 1, :] > 0.5, st, NEG)
            if n == i and mode != "dil":
                st = jnp.where(causal, st, NEG)
            s_scr[slot, n] = st
            m8 = jnp.maximum(m8, _fold8(st, jnp.maximum))
        m = jnp.max(m8, axis=0, keepdims=True)
        if mode == "fox":
            ct = crow_ref[0, slot, :, r0:r0 + ATT_BLK]
            shift = ct - (m + ct)
        else:
            shift = -m

        l8 = jnp.zeros((8, ATT_BLK), F32)
        acc_t = jnp.zeros((HEAD_DIM, ATT_BLK), F32)
        v_rows = slice(slot * HEAD_DIM, (slot + 1) * HEAD_DIM)
        for n in range(i + 1):
            c0 = n * ATT_BLK
            p = jnp.exp2(s_scr[slot, n] + shift)
            l8 = l8 + _fold8(p, jnp.add)
            acc_t = acc_t + _dot(vt_ref[0, v_rows, c0:c0 + ATT_BLK], p.astype(BF16))
        return acc_t * (1.0 / jnp.sum(l8, axis=0, keepdims=True))

    def run(slots):
        hmasks = [(lane >= s * HEAD_DIM) & (lane < (s + 1) * HEAD_DIM) for s in (0, 1)]
        km3 = [block_means(s, hmasks[s]) if (mode == "moba" and s in slots) else None for s in (0, 1)]
        for i in range(N_ATT_BLK):
            parts = [query_block(s, i, hmasks[s], km3[s]) if s in slots
                     else jnp.zeros((HEAD_DIM, ATT_BLK), F32) for s in (0, 1)]
            o_ref[0, i * ATT_BLK:(i + 1) * ATT_BLK, :] = jnp.concatenate(parts, axis=0).T

    if h_first % 2 == 0 and h_last % 2 == 1:
        run((0, 1))
    else:
        both = (2 * pair >= h_first) & (2 * pair + 1 <= h_last)
        lone = (1,) if h_first % 2 == 1 else (0,)

        @pl.when(both)
        def _():
            run((0, 1))

        @pl.when(jnp.logical_not(both))
        def _():
            run(lone)


def _attention(mode, h_first, n_heads, qk3, vt3, extras):
    B = qk3.shape[0]
    n_pairs = N_HEADS // 2
    h_last = h_first + n_heads - 1
    pair0 = h_first // 2
    n_out_pairs = h_last // 2 - pair0 + 1

    def col(part):
        return lambda b, j: (b, 0, part * n_pairs + pair0 + j)

    in_specs = [pl.BlockSpec((1, SEQ, LANE), col(0)), pl.BlockSpec((1, SEQ, LANE), col(1)),
                pl.BlockSpec((1, LANE, SEQ), lambda b, j: (b, pair0 + j, 0))]
    scratch = [pltpu.VMEM((2, N_ATT_BLK, ATT_BLK, ATT_BLK), F32)]
    if mode == "fox":
        in_specs += [pl.BlockSpec((1, 2, SEQ, LANE), lambda b, j: (b, j, 0, 0)),
                     pl.BlockSpec((1, 2, 1, SEQ), lambda b, j: (b, j, 0, 0))]
    elif mode == "moba":
        scratch += [pltpu.VMEM((2, LANE, LANE), F32)]
    else:
        in_specs += [pl.BlockSpec((N_ATT_BLK, ATT_BLK, ATT_BLK), lambda b, j: (0, 0, 0))]
    return pl.pallas_call(
        functools.partial(_attn_kernel, mode, h_first, h_last),
        grid=(B, n_out_pairs),
        in_specs=in_specs,
        out_specs=pl.BlockSpec((1, SEQ, LANE), lambda b, j: (b, 0, j)),
        out_shape=jax.ShapeDtypeStruct((B, SEQ, n_out_pairs * LANE), F32),
        scratch_shapes=scratch,
        compiler_params=_cparams(("parallel", "parallel")),
        name="attn_" + mode,
    )(qk3, qk3, vt3, *extras)


def _dilated_log2_multiplicity_t():
    r = np.arange(ATT_BLK)[None, :]
    c = np.arange(ATT_BLK)[:, None]
    out = np.empty((N_ATT_BLK, ATT_BLK, ATT_BLK), np.float32)
    for d in range(N_ATT_BLK):
        dist = d * ATT_BLK + r - c
        mult = np.zeros_like(dist)
        for window, dil in DIL_PATTERNS:
            mult += ((dist >= 0) & (dist % dil == 0) & (dist // dil <= window // dil)).astype(dist.dtype)
        out[d] = np.where(mult > 0, np.log2(np.maximum(mult, 1)), NEG)
    return out


OUT_TM = 512
MIX_W = 3 * LANE


def _outproj_kernel(x_ref, oa_ref, ob_ref, oc_ref, ga_ref, gb_ref, gc_ref, wa_ref, wb_ref, wc_ref, o_ref):
    def normed(o_r, g_r, width):
        o = o_r[...]
        ms = jnp.sum(o * o, axis=-1, keepdims=True) * (1.0 / width)
        return (o * lax.rsqrt(ms + EPS) * g_r[...]).astype(BF16)

    ya = normed(oa_ref, ga_ref, H_FOX * HEAD_DIM)
    yb = normed(ob_ref, gb_ref, H_MOBA * HEAD_DIM)
    yc = normed(oc_ref, gc_ref, H_DIL * HEAD_DIM)
    o_ref[...] = x_ref[...] + _dot(ya, wa_ref[...]) + _dot(yb, wb_ref[...]) + _dot(yc, wc_ref[...])


def _outproj(x2, oa, ob, oc, ga, gb, gc, wa, wb, wc):
    T = x2.shape[0]
    row = lambda w: pl.BlockSpec((OUT_TM, w), lambda i: (i, 0))
    full = lambda a, b: pl.BlockSpec((a, b), lambda i: (0, 0))
    return pl.pallas_call(
        _outproj_kernel,
        grid=(T // OUT_TM,),
        in_specs=[row(D_MODEL), row(MIX_W), row(MIX_W), row(MIX_W),
                  full(1, MIX_W), full(1, MIX_W), full(1, MIX_W),
                  full(MIX_W, D_MODEL), full(MIX_W, D_MODEL), full(MIX_W, D_MODEL)],
        out_specs=row(D_MODEL),
        out_shape=jax.ShapeDtypeStruct((T, D_MODEL), F32),
        compiler_params=_cparams(("parallel",)),
        name="outproj",
    )(x2, oa, ob, oc, ga, gb, gc, wa, wb, wc)


FFN_TM = 1024
FFN_TF = 512
FFN_SUB = 256


def _swiglu_act(a, b):
    return a * (1.0 / (1.0 + jnp.exp(-a))) * b


def _ffn_kernel(x_ref, g_ref, w1_ref, w3_ref, w2_ref, o_ref, h_scr):
    f = pl.program_id(1)

    @pl.when(f == 0)
    def _():
        x = x_ref[...]
        h_scr[...] = _rms(x, g_ref[...]).astype(BF16)
        o_ref[...] = x

    for c in range(FFN_TM // FFN_SUB):
        rows = slice(c * FFN_SUB, (c + 1) * FFN_SUB)
        h = h_scr[rows, :]
        act = _swiglu_act(_dot(h, w1_ref[...]), _dot(h, w3_ref[...])).astype(BF16)
        o_ref[rows, :] = o_ref[rows, :] + _dot(act, w2_ref[...])


def _ffn(x2, g, w1, w3, w2):
    T = x2.shape[0]
    return pl.pallas_call(
        _ffn_kernel,
        grid=(T // FFN_TM, D_FF // FFN_TF),
        in_specs=[
            pl.BlockSpec((FFN_TM, D_MODEL), lambda i, f: (i, 0)),
            pl.BlockSpec((1, D_MODEL), lambda i, f: (0, 0)),
            pl.BlockSpec((D_MODEL, FFN_TF), lambda i, f: (0, f)),
            pl.BlockSpec((D_MODEL, FFN_TF), lambda i, f: (0, f)),
            pl.BlockSpec((FFN_TF, D_MODEL), lambda i, f: (f, 0)),
        ],
        out_specs=pl.BlockSpec((FFN_TM, D_MODEL), lambda i, f: (i, 0)),
        out_shape=jax.ShapeDtypeStruct((T, D_MODEL), F32),
        scratch_shapes=[pltpu.VMEM((FFN_TM, D_MODEL), BF16)],
        compiler_params=_cparams(("parallel", "arbitrary")),
        name="ffn_dense",
    )(x2, g, w1, w3, w2)


MOE_TT = 2048
MOE_CH = 256
MOE_UNIT = 128
MOE_TF = 512
MOE_KB = 256
MOE_NKB = MOE_TT // MOE_KB


def _router_kernel(x_ref, g_ref, wr_hi_ref, wr_lo_ref, tri_ref,
                   h_ref, posc_ref, gatec_ref, posr_ref, cnt_ref):
    h = _rms(x_ref[...], g_ref[...])
    h_hi = h.astype(BF16)
    h_lo = (h - h_hi.astype(F32)).astype(BF16)
    h_ref[...] = h_hi
    wr_hi, wr_lo = wr_hi_ref[...], wr_lo_ref[...]
    logits = _dot(h_hi, wr_hi) + _dot(h_lo, wr_hi) + _dot(h_hi, wr_lo)
    lane = lax.broadcasted_iota(jnp.int32, (1, LANE), 1).astype(F32)
    lg = jnp.where(lane < N_EXPERTS, logits, -jnp.inf)
    m1 = jnp.max(lg, axis=1, keepdims=True)
    i1 = jnp.min(jnp.where(lg == m1, lane, float(LANE)), axis=1, keepdims=True)
    lg2 = jnp.where(lane == i1, -jnp.inf, lg)
    m2 = jnp.max(lg2, axis=1, keepdims=True)
    i2 = jnp.min(jnp.where(lg2 == m2, lane, float(LANE)), axis=1, keepdims=True)
    e2 = jnp.exp(m2 - m1)
    g1 = 1.0 / (1.0 + e2)
    g2 = e2 * g1
    routed = (lane == i1) | (lane == i2)
    gatec_ref[...] = jnp.where(lane == i1, g1, jnp.where(lane == i2, g2, 0.0))
    routed_f = jnp.where(routed, 1.0, 0.0)
    blocks = _cumsum_rows(routed_f, tri_ref[...], exact_small_ints=True)
    for b, cnt in enumerate(blocks):
        sl = slice(b * LANE, (b + 1) * LANE)
        pos = jnp.where(routed_f[sl] > 0.5, cnt - 1.0, -1.0)
        posc_ref[sl, :] = pos
        posr_ref[0, :, sl] = pos.T[:8, :]
    per_kb = MOE_KB // LANE
    for kb in range(MOE_NKB):
        cnt_ref[0, kb:kb + 1, :] = blocks[(kb + 1) * per_kb - 1][LANE - 1:LANE, :]


def _router(x2, g, wr_hi, wr_lo, tri):
    T = x2.shape[0]
    nt = T // MOE_TT
    return pl.pallas_call(
        _router_kernel,
        grid=(nt,),
        in_specs=[
            pl.BlockSpec((MOE_TT, D_MODEL), lambda i: (i, 0)),
            pl.BlockSpec((1, D_MODEL), lambda i: (0, 0)),
            pl.BlockSpec((D_MODEL, LANE), lambda i: (0, 0)),
            pl.BlockSpec((D_MODEL, LANE), lambda i: (0, 0)),
            pl.BlockSpec((LANE, LANE), lambda i: (0, 0)),
        ],
        out_specs=[
            pl.BlockSpec((MOE_TT, D_MODEL), lambda i: (i, 0)),
            pl.BlockSpec((MOE_TT, LANE), lambda i: (i, 0)),
            pl.BlockSpec((MOE_TT, LANE), lambda i: (i, 0)),
            pl.BlockSpec((1, 8, MOE_TT), lambda i: (i, 0, 0)),
            pl.BlockSpec((1, MOE_NKB, LANE), lambda i: (i, 0, 0)),
        ],
        out_shape=[
            jax.ShapeDtypeStruct((T, D_MODEL), BF16),
            jax.ShapeDtypeStruct((T, LANE), F32),
            jax.ShapeDtypeStruct((T, LANE), F32),
            jax.ShapeDtypeStruct((nt, 8, MOE_TT), F32),
            jax.ShapeDtypeStruct((nt, MOE_NKB, LANE), F32),
        ],
        compiler_params=_cparams(("parallel",)),
        name="moe_router",
    )(x2, g, wr_hi, wr_lo, tri)


def _moe_kernel(pref_ref, h_ref, posc_ref, gatec_ref, posr_ref, w1_ref, w3_ref, w2_ref, o_ref,
                xe_scr, ye_scr, pe_scr, ge_scr, acc_scr):
    t, e, f = pl.program_id(0), pl.program_id(1), pl.program_id(2)
    nf = pl.num_programs(2)
    base = (t * N_EXPERTS + e) * (MOE_NKB + 1)
    pref = [pref_ref[base + kb] for kb in range(MOE_NKB + 1)]
    n_rows = pref[MOE_NKB]
    lane = lax.broadcasted_iota(jnp.int32, (1, LANE), 1)

    @pl.when((e == 0) & (f == 0))
    def _():
        o_ref[...] = jnp.zeros((MOE_TT, D_MODEL), F32)

    @pl.when(f == 0)
    def _():
        is_e = lane == e
        pe_scr[...] = jnp.sum(jnp.where(is_e, posc_ref[...], 0.0), axis=1, keepdims=True)
        ge_scr[...] = jnp.sum(jnp.where(is_e, gatec_ref[...], 0.0), axis=1, keepdims=True)
        pos_row = posr_ref[0, pl.ds(e, 1), :]
        slot = lax.broadcasted_iota(jnp.int32, (MOE_CH, 1), 0).astype(F32)

        def gather(c, _):
            r0 = pl.multiple_of(c * MOE_CH, MOE_CH)
            rows = pl.ds(r0, MOE_CH)
            slots = slot + r0.astype(F32)
            acc_scr[...] = jnp.zeros((MOE_CH, D_MODEL), F32)
            ye_scr[rows, :] = jnp.zeros((MOE_CH, D_MODEL), F32)
            for kb in range(MOE_NKB):
                @pl.when((pref[kb] < r0 + MOE_CH) & (pref[kb + 1] > r0))
                def _():
                    tok = slice(kb * MOE_KB, (kb + 1) * MOE_KB)
                    onehot = jnp.where(pos_row[:, tok] == slots, 1.0, 0.0).astype(BF16)
                    acc_scr[...] = acc_scr[...] + _dot(onehot, h_ref[tok, :])
            xe_scr[rows, :] = acc_scr[...].astype(BF16)
            return 0

        lax.fori_loop(0, (n_rows + MOE_CH - 1) // MOE_CH, gather, 0)

    def expert_rows(r0, n):
        rows = pl.ds(r0, n)
        xc = xe_scr[rows, :]
        act = _swiglu_act(_dot(xc, w1_ref[0]), _dot(xc, w3_ref[0])).astype(BF16)
        ye_scr[rows, :] = ye_scr[rows, :] + _dot(act, w2_ref[0])

    n_units = (n_rows + MOE_UNIT - 1) // MOE_UNIT
    units_per_trip = 2 * MOE_CH // MOE_UNIT

    def two_chunks(i, _):
        r0 = pl.multiple_of(i * (2 * MOE_CH), 2 * MOE_CH)
        expert_rows(r0, MOE_CH)
        expert_rows(r0 + MOE_CH, MOE_CH)
        return 0

    n_trips = n_units // units_per_trip
    lax.fori_loop(0, n_trips, two_chunks, 0)
    left = n_units - n_trips * units_per_trip
    tail0 = pl.multiple_of(n_trips * (2 * MOE_CH), 2 * MOE_CH)

    @pl.when(left >= 2)
    def _():
        expert_rows(tail0, MOE_CH)

    @pl.when(left == 1)
    def _():
        expert_rows(tail0, MOE_UNIT)

    @pl.when(left == 3)
    def _():
        expert_rows(tail0 + MOE_CH, MOE_UNIT)

    @pl.when(f == nf - 1)
    def _():
        slot_l = lax.broadcasted_iota(jnp.int32, (1, MOE_CH), 1).astype(F32)
        for kb in range(MOE_NKB):
            lo, hi = pref[kb], pref[kb + 1]

            @pl.when(hi > lo)
            def _():
                tok = slice(kb * MOE_KB, (kb + 1) * MOE_KB)
                pe_b = pe_scr[tok, :]
                ge_b = ge_scr[tok, :]

                def combine(c, _):
                    r0 = pl.multiple_of(c * MOE_CH, MOE_CH)
                    onehot_t = jnp.where(pe_b == slot_l + r0.astype(F32), 1.0, 0.0).astype(BF16)
                    y = ye_scr[pl.ds(r0, MOE_CH), :].astype(BF16)
                    o_ref[tok, :] = o_ref[tok, :] + ge_b * _dot(onehot_t, y)
                    return 0

                lax.fori_loop(lo // MOE_CH, (hi - 1) // MOE_CH + 1, combine, 0)


def _moe(counts, h, posc, gatec, posr, we1, we3, we2):
    T = h.shape[0]
    nt = T // MOE_TT
    grid_spec = pltpu.PrefetchScalarGridSpec(
        num_scalar_prefetch=1,
        grid=(nt, N_EXPERTS, D_FF // MOE_TF),
        in_specs=[
            pl.BlockSpec((MOE_TT, D_MODEL), lambda t, e, f, c: (t, 0)),
            pl.BlockSpec((MOE_TT, LANE), lambda t, e, f, c: (t, 0)),
            pl.BlockSpec((MOE_TT, LANE), lambda t, e, f, c: (t, 0)),
            pl.BlockSpec((1, 8, MOE_TT), lambda t, e, f, c: (t, 0, 0)),
            pl.BlockSpec((1, D_MODEL, MOE_TF), lambda t, e, f, c: (e, 0, f)),
            pl.BlockSpec((1, D_MODEL, MOE_TF), lambda t, e, f, c: (e, 0, f)),
            pl.BlockSpec((1, MOE_TF, D_MODEL), lambda t, e, f, c: (e, f, 0)),
        ],
        out_specs=pl.BlockSpec((MOE_TT, D_MODEL), lambda t, e, f, c: (t, 0)),
        scratch_shapes=[
            pltpu.VMEM((MOE_TT, D_MODEL), BF16),
            pltpu.VMEM((MOE_TT, D_MODEL), F32),
            pltpu.VMEM((MOE_TT, 1), F32),
            pltpu.VMEM((MOE_TT, 1), F32),
            pltpu.VMEM((MOE_CH, D_MODEL), F32),
        ],
    )
    return pl.pallas_call(
        _moe_kernel,
        grid_spec=grid_spec,
        out_shape=jax.ShapeDtypeStruct((T, D_MODEL), F32),
        compiler_params=_cparams(("parallel", "arbitrary", "arbitrary")),
        name="moe_experts",
    )(counts, h, posc, gatec, posr, we1, we3, we2)


FIN_TM = 1024


def _final_kernel(x_ref, y_ref, g_ref, o_ref):
    o_ref[...] = _rms(x_ref[...] + y_ref[...], g_ref[...])


def _final(x2, y2, g):
    T = x2.shape[0]
    row = pl.BlockSpec((FIN_TM, D_MODEL), lambda i: (i, 0))
    return pl.pallas_call(
        _final_kernel,
        grid=(T // FIN_TM,),
        in_specs=[row, row, pl.BlockSpec((1, D_MODEL), lambda i: (0, 0))],
        out_specs=row,
        out_shape=jax.ShapeDtypeStruct((T, D_MODEL), F32),
        compiler_params=_cparams(("parallel",)),
        name="final_norm",
    )(x2, y2, g)


def _rope_lane_tables():
    half = ROT_DIM // 2
    inv = ROPE_THETA ** (-jnp.arange(0, ROT_DIM, 2, dtype=F32) / ROT_DIM)
    ang = jnp.arange(SEQ, dtype=F32)[:, None] * inv[None, :]
    cos, sin = jnp.cos(ang), jnp.sin(ang)
    ones = jnp.ones((SEQ, HEAD_DIM - ROT_DIM), F32)
    zeros = jnp.zeros((SEQ, HEAD_DIM - ROT_DIM), F32)
    z8 = jnp.zeros((SEQ, half), F32)
    cos_h = jnp.concatenate([cos, cos, ones], axis=1)
    sp_h = jnp.concatenate([z8, sin, zeros], axis=1)
    sm_h = jnp.concatenate([-sin, z8, zeros], axis=1)
    two = lambda a: jnp.concatenate([a, a], axis=1)
    return two(cos_h), two(sp_h), two(sm_h)


def _pad_lanes(a, width=LANE):
    return jnp.pad(a, ((0, 0), (0, width - a.shape[1])))


def _mixer(x2, B, norm_mix, w_in, b_f, mix_gain, w_out, tables, tri, lm):
    w_qk = w_in[:, :QK_W].astype(BF16)
    w_vt = w_in[:, QK_W:QKV_W].T.astype(BF16)
    w_f = _pad_lanes(w_in[:, QKV_W:]).astype(BF16)
    qk, vt3, f_logit = _inproj(x2, norm_mix[None, :], w_qk, w_vt, w_f, *tables)
    cbc, crow = _forget_cumsum(f_logit.reshape(B, SEQ, LANE), _pad_lanes(b_f[None, :]), tri)
    qk3 = qk.reshape(B, SEQ, QK_W)
    oa = _attention("fox", 0, H_FOX, qk3, vt3, (cbc, crow))
    ob = _attention("moba", H_FOX, H_MOBA, qk3, vt3, ())
    oc = _attention("dil", H_FOX + H_MOBA, H_DIL, qk3, vt3, (lm,))
    a0, b0, c0 = 0, (H_FOX // 2) * PAIR, ((H_FOX + H_MOBA) // 2) * PAIR
    T = x2.shape[0]
    sl = lambda v, s: v[s:s + MIX_W]
    return _outproj(
        x2, oa.reshape(T, MIX_W), ob.reshape(T, MIX_W), oc.reshape(T, MIX_W),
        sl(mix_gain, a0)[None, :], sl(mix_gain, b0)[None, :], sl(mix_gain, c0)[None, :],
        sl(w_out, a0).astype(BF16), sl(w_out, b0).astype(BF16), sl(w_out, c0).astype(BF16))


def kernel(x, l0_norm_mix, l0_w_in, l0_b_f, l0_mix_gain, l0_w_out, l0_norm_ffn, l0_w1, l0_w3, l0_w2, l1_norm_mix, l1_w_in, l1_b_f, l1_mix_gain, l1_w_out, l1_norm_ffn, l1_w_router, l1_we1, l1_we3, l1_we2, final_norm):
    B, S, D = x.shape
    assert (S, D) == (SEQ, D_MODEL)
    T = B * S
    tables = _rope_lane_tables()
    tri = jnp.asarray(np.tril(np.ones((LANE, LANE), np.float32)), BF16)
    lm = jnp.asarray(_dilated_log2_multiplicity_t())

    x2 = x.reshape(T, D)
    x2 = _mixer(x2, B, l0_norm_mix, l0_w_in, l0_b_f, l0_mix_gain, l0_w_out, tables, tri, lm)
    x2 = _ffn(x2, l0_norm_ffn[None, :], l0_w1.astype(BF16), l0_w3.astype(BF16), l0_w2.astype(BF16))
    x2 = _mixer(x2, B, l1_norm_mix, l1_w_in, l1_b_f, l1_mix_gain, l1_w_out, tables, tri, lm)

    wr = _pad_lanes(l1_w_router)
    wr_hi = wr.astype(BF16)
    wr_lo = (wr - wr_hi.astype(F32)).astype(BF16)
    h, posc, gatec, posr, cnt = _router(x2, l1_norm_ffn[None, :], wr_hi, wr_lo, tri)
    ends = cnt[:, :, :N_EXPERTS].astype(jnp.int32).transpose(0, 2, 1)
    pref = jnp.pad(ends, ((0, 0), (0, 0), (1, 0))).reshape(-1)
    y2 = _moe(pref, h, posc, gatec, posr,
              l1_we1.astype(BF16), l1_we3.astype(BF16), l1_we2.astype(BF16))
    return _final(x2, y2, final_norm[None, :]).reshape(B, S, D)
```

```python
import functools
import math

import numpy as np
import jax
import jax.numpy as jnp
from jax import lax
from jax.experimental import pallas as pl
from jax.experimental.pallas import tpu as pltpu

D_MODEL = 1024
SEQ = 2048
HEAD_DIM = 64
N_HEADS = 16
H_FOX = 5
H_MOBA = 5
H_DIL = 6
ROT_DIM = 16
ROPE_THETA = 500000.0
MOBA_BLOCK = 256
MOBA_TOPK = 3
DIL_PATTERNS = ((128, 1), (512, 4), (2048, 16))
D_FF = 3584
N_EXPERTS = 8
EPS = 1e-6

LANE = 128
PAIR = 2 * HEAD_DIM
MIX = N_HEADS * HEAD_DIM
FOX_SLOTS = 2 * ((H_FOX + 1) // 2)
QK_W = 2 * MIX
QKV_W = 3 * MIX
ATT_BLK = 256
N_ATT_BLK = SEQ // ATT_BLK
NEG = -1e30
LOG2E = math.log2(math.e)
Q_SCALE = HEAD_DIM ** -0.5 * LOG2E
VMEM_LIMIT = 56 * 1024 * 1024

BF16 = jnp.bfloat16
F32 = jnp.float32


def _dot(a, b):
    return jnp.dot(a, b, preferred_element_type=F32)


def _dot_nt(a, b):
    return lax.dot_general(a, b, (((1,), (1,)), ((), ())), preferred_element_type=F32)


def _split3(x):
    hi = x.astype(BF16)
    r1 = x - hi.astype(F32)
    mid = r1.astype(BF16)
    lo = (r1 - mid.astype(F32)).astype(BF16)
    return hi, mid, lo


def _rms(x, g):
    return x * lax.rsqrt(jnp.mean(x * x, axis=-1, keepdims=True) + EPS) * g


def _cparams(sem):
    return pltpu.CompilerParams(dimension_semantics=sem, vmem_limit_bytes=VMEM_LIMIT)


IN_TM = 512
IN_TN = 512


def _inproj_kernel(x_ref, g_ref, w_ref, wvt_ref, wf_ref, cos_ref, sp_ref, sm_ref, qk_ref, vt_ref, f_ref):
    hb = _rms(x_ref[...], g_ref[...]).astype(BF16)
    f_ref[...] = _dot(hb, wf_ref[...])
    vt_ref[0] = _dot_nt(wvt_ref[...], hb).astype(BF16)
    cos, sp, sm = cos_ref[...], sp_ref[...], sm_ref[...]
    lane = lax.broadcasted_iota(jnp.int32, (1, LANE), 1)
    upper = lane >= HEAD_DIM
    n_pairs = N_HEADS // 2
    first_rot_head = H_FOX
    for c in range(QK_W // IN_TN):
        y = _dot(hb, w_ref[:, c * IN_TN:(c + 1) * IN_TN])
        for t in range(IN_TN // LANE):
            col = c * (IN_TN // LANE) + t
            part, pair = divmod(col, n_pairs)
            yt = y[:, t * LANE:(t + 1) * LANE]
            lo_head, hi_head = 2 * pair, 2 * pair + 1
            if hi_head >= first_rot_head:
                c_t, sp_t, sm_t = cos, sp, sm
                if lo_head < first_rot_head:
                    c_t = jnp.where(upper, cos, 1.0)
                    sp_t = jnp.where(upper, sp, 0.0)
                    sm_t = jnp.where(upper, sm, 0.0)
                half = ROT_DIM // 2
                yt = (yt * c_t + pltpu.roll(yt, half, axis=1) * sp_t
                      + pltpu.roll(yt, LANE - half, axis=1) * sm_t)
            if part == 0:
                yt = yt * Q_SCALE
            qk_ref[:, col * LANE:(col + 1) * LANE] = yt.astype(BF16)


def _inproj(x2, g, w_qk, w_vt, w_f, cos_t, sp_t, sm_t):
    T = x2.shape[0]
    n_pos = SEQ // IN_TM
    tab = pl.BlockSpec((IN_TM, LANE), lambda i: (i % n_pos, 0))
    return pl.pallas_call(
        _inproj_kernel,
        grid=(T // IN_TM,),
        in_specs=[
            pl.BlockSpec((IN_TM, D_MODEL), lambda i: (i, 0)),
            pl.BlockSpec((1, D_MODEL), lambda i: (0, 0)),
            pl.BlockSpec((D_MODEL, QK_W), lambda i: (0, 0)),
            pl.BlockSpec((MIX, D_MODEL), lambda i: (0, 0)),
            pl.BlockSpec((D_MODEL, LANE), lambda i: (0, 0)),
            tab, tab, tab,
        ],
        out_specs=[
            pl.BlockSpec((IN_TM, QK_W), lambda i: (i, 0)),
            pl.BlockSpec((1, MIX, IN_TM), lambda i: (i // n_pos, 0, i % n_pos)),
            pl.BlockSpec((IN_TM, LANE), lambda i: (i, 0)),
        ],
        out_shape=[
            jax.ShapeDtypeStruct((T, QK_W), BF16),
            jax.ShapeDtypeStruct((T // SEQ, MIX, SEQ), BF16),
            jax.ShapeDtypeStruct((T, LANE), F32),
        ],
        compiler_params=_cparams(("parallel",)),
        name="inproj",
    )(x2, g, w_qk, w_vt, w_f, cos_t, sp_t, sm_t)


def _cumsum_rows(x, tri, exact_small_ints):
    n = x.shape[0]
    carry = jnp.zeros((1, LANE), F32)
    out = []
    for b in range(n // LANE):
        xb = x[b * LANE:(b + 1) * LANE]
        if exact_small_ints:
            y = _dot(tri, xb.astype(BF16))
        else:
            hi, mid, lo = _split3(xb)
            y = _dot(tri, hi) + _dot(tri, mid) + _dot(tri, lo)
        y = y + carry
        carry = y[LANE - 1:LANE, :]
        out.append(y)
    return out


def _forget_kernel(f_ref, bf_ref, tri_ref, cbc_ref, crow_ref):
    z = f_ref[0] + bf_ref[...]
    logf = (jnp.minimum(z, 0.0) - jnp.log(1.0 + jnp.exp(-jnp.abs(z)))) * LOG2E
    blocks = _cumsum_rows(logf, tri_ref[...], exact_small_ints=False)
    for b, y in enumerate(blocks):
        yt = y.T
        for h in range(FOX_SLOTS):
            live = h < H_FOX
            cbc_ref[0, h, b * LANE:(b + 1) * LANE, :] = (
                jnp.broadcast_to(y[:, h:h + 1], (LANE, LANE)) if live else jnp.zeros((LANE, LANE), F32))
            crow_ref[0, h, :, b * LANE:(b + 1) * LANE] = yt[h:h + 1, :] if live else jnp.zeros((1, LANE), F32)


def _forget_cumsum(f3, bf_pad, tri):
    B = f3.shape[0]
    return pl.pallas_call(
        _forget_kernel,
        grid=(B,),
        in_specs=[
            pl.BlockSpec((1, SEQ, LANE), lambda b: (b, 0, 0)),
            pl.BlockSpec((1, LANE), lambda b: (0, 0)),
            pl.BlockSpec((LANE, LANE), lambda b: (0, 0)),
        ],
        out_specs=[
            pl.BlockSpec((1, FOX_SLOTS, SEQ, LANE), lambda b: (b, 0, 0, 0)),
            pl.BlockSpec((1, FOX_SLOTS, 1, SEQ), lambda b: (b, 0, 0, 0)),
        ],
        out_shape=[
            jax.ShapeDtypeStruct((B, FOX_SLOTS, SEQ, LANE), F32),
            jax.ShapeDtypeStruct((B, FOX_SLOTS, 1, SEQ), F32),
        ],
        compiler_params=_cparams(("parallel",)),
        name="forget_cumsum",
    )(f3, bf_pad, tri)


def _fold8(x, op):
    parts = [x[r:r + 8] for r in range(0, x.shape[0], 8)]
    while len(parts) > 1:
        parts = [op(parts[a], parts[a + 1]) for a in range(0, len(parts) - 1, 2)] + parts[len(parts) & ~1:]
    return parts[0]


def _attn_kernel(mode, h_first, h_last, q_ref, k_ref, vt_ref, *rest):
    if mode == "fox":
        cbc_ref, crow_ref, o_ref, s_scr = rest
    elif mode == "moba":
        o_ref, s_scr, km_scr = rest
    else:
        lmt_ref, o_ref, s_scr = rest
    pair = h_first // 2 + pl.program_id(1)
    lane = lax.broadcasted_iota(jnp.int32, (1, LANE), 1)
    key_idx = lax.broadcasted_iota(jnp.int32, (ATT_BLK, ATT_BLK), 0)
    qry_idx = lax.broadcasted_iota(jnp.int32, (ATT_BLK, ATT_BLK), 1)
    causal = key_idx <= qry_idx
    sub8 = lax.broadcasted_iota(jnp.int32, (8, ATT_BLK), 0)

    def block_means(slot, hmask):
        km_scr[slot] = jnp.zeros((LANE, LANE), F32)
        for n in range(N_ATT_BLK):
            kb = k_ref[0, n * ATT_BLK:(n + 1) * ATT_BLK, :].astype(F32)
            km = jnp.sum(kb, axis=0, keepdims=True) * (1.0 / MOBA_BLOCK)
            km_scr[slot, n:n + 1, :] = jnp.where(hmask, km, 0.0)
        return _split3(km_scr[slot])

    def query_block(slot, i, hmask, km3):
        r0 = i * ATT_BLK
        qm = jnp.where(hmask, q_ref[0, r0:r0 + ATT_BLK, :], 0.0).astype(BF16)
        if mode == "moba" and i > MOBA_TOPK:
            gt = (_dot_nt(km3[0], qm) + _dot_nt(km3[1], qm) + _dot_nt(km3[2], qm))[:8, :]
            rank = jnp.zeros((8, ATT_BLK), F32)
            for n2 in range(i):
                g2 = gt[n2:n2 + 1, :]
                ahead = (g2 > gt) | ((g2 == gt) & (n2 < sub8))
                rank = rank + jnp.where(ahead, 1.0, 0.0)
            sel_t = jnp.where((rank < MOBA_TOPK) & (sub8 < i), 1.0, 0.0)

        m8 = jnp.full((8, ATT_BLK), NEG, F32)
        for n in range(i + 1):
            c0 = n * ATT_BLK
            st = _dot_nt(k_ref[0, c0:c0 + ATT_BLK, :], qm)
            if mode == "fox":
                cs = cbc_ref[0, slot, c0:c0 + ATT_BLK, :]
                st = st - jnp.concatenate([cs, cs], axis=1)
            elif mode == "dil":
                st = st + lmt_ref[i - n]
            elif n < i and i > MOBA_TOPK:
                st = jnp.where(sel_t[n:n + 1, :] > 0.5, st, NEG)
            if n == i and mode != "dil":
                st = jnp.where(causal, st, NEG)
            s_scr[slot, n] = st
            m8 = jnp.maximum(m8, _fold8(st, jnp.maximum))
        m = jnp.max(m8, axis=0, keepdims=True)
        if mode == "fox":
            ct = crow_ref[0, slot, :, r0:r0 + ATT_BLK]
            shift = ct - (m + ct)
        else:
            shift = -m

        l8 = jnp.zeros((8, ATT_BLK), F32)
        acc_t = jnp.zeros((HEAD_DIM, ATT_BLK), F32)
        v_rows = slice(slot * HEAD_DIM, (slot + 1) * HEAD_DIM)
        for n in range(i + 1):
            c0 = n * ATT_BLK
            p = jnp.exp2(s_scr[slot, n] + shift)
            l8 = l8 + _fold8(p, jnp.add)
            acc_t = acc_t + _dot(vt_ref[0, v_rows, c0:c0 + ATT_BLK], p.astype(BF16))
        return acc_t * (1.0 / jnp.sum(l8, axis=0, keepdims=True))

    def run(slots):
        hmasks = [(lane >= s * HEAD_DIM) & (lane < (s + 1) * HEAD_DIM) for s in (0, 1)]
        km3 = [block_means(s, hmasks[s]) if (mode == "moba" and s in slots) else None for s in (0, 1)]
        for i in range(N_ATT_BLK):
            parts = [query_block(s, i, hmasks[s], km3[s]) if s in slots
                     else jnp.zeros((HEAD_DIM, ATT_BLK), F32) for s in (0, 1)]
            o_ref[0, i * ATT_BLK:(i + 1) * ATT_BLK, :] = jnp.concatenate(parts, axis=0).T

    if h_first % 2 == 0 and h_last % 2 == 1:
        run((0, 1))
    else:
        both = (2 * pair >= h_first) & (2 * pair + 1 <= h_last)
        lone = (1,) if h_first % 2 == 1 else (0,)

        @pl.when(both)
        def _():
            run((0, 1))

        @pl.when(jnp.logical_not(both))
        def _():
            run(lone)


def _attention(mode, h_first, n_heads, qk3, vt3, extras):
    B = qk3.shape[0]
    n_pairs = N_HEADS // 2
    h_last = h_first + n_heads - 1
    pair0 = h_first // 2
    n_out_pairs = h_last // 2 - pair0 + 1

    def col(part):
        return lambda b, j: (b, 0, part * n_pairs + pair0 + j)

    in_specs = [pl.BlockSpec((1, SEQ, LANE), col(0)), pl.BlockSpec((1, SEQ, LANE), col(1)),
                pl.BlockSpec((1, LANE, SEQ), lambda b, j: (b, pair0 + j, 0))]
    scratch = [pltpu.VMEM((2, N_ATT_BLK, ATT_BLK, ATT_BLK), F32)]
    if mode == "fox":
        in_specs += [pl.BlockSpec((1, 2, SEQ, LANE), lambda b, j: (b, j, 0, 0)),
                     pl.BlockSpec((1, 2, 1, SEQ), lambda b, j: (b, j, 0, 0))]
    elif mode == "moba":
        scratch += [pltpu.VMEM((2, LANE, LANE), F32)]
    else:
        in_specs += [pl.BlockSpec((N_ATT_BLK, ATT_BLK, ATT_BLK), lambda b, j: (0, 0, 0))]
    return pl.pallas_call(
        functools.partial(_attn_kernel, mode, h_first, h_last),
        grid=(B, n_out_pairs),
        in_specs=in_specs,
        out_specs=pl.BlockSpec((1, SEQ, LANE), lambda b, j: (b, 0, j)),
        out_shape=jax.ShapeDtypeStruct((B, SEQ, n_out_pairs * LANE), F32),
        scratch_shapes=scratch,
        compiler_params=_cparams(("parallel", "parallel")),
        name="attn_" + mode,
    )(qk3, qk3, vt3, *extras)


def _dilated_log2_multiplicity_t():
    r = np.arange(ATT_BLK)[None, :]
    c = np.arange(ATT_BLK)[:, None]
    out = np.empty((N_ATT_BLK, ATT_BLK, ATT_BLK), np.float32)
    for d in range(N_ATT_BLK):
        dist = d * ATT_BLK + r - c
        mult = np.zeros_like(dist)
        for window, dil in DIL_PATTERNS:
            mult += ((dist >= 0) & (dist % dil == 0) & (dist // dil <= window // dil)).astype(dist.dtype)
        out[d] = np.where(mult > 0, np.log2(np.maximum(mult, 1)), NEG)
    return out


OUT_TM = 512
MIX_W = 3 * LANE


def _outproj_kernel(x_ref, oa_ref, ob_ref, oc_ref, ga_ref, gb_ref, gc_ref, wa_ref, wb_ref, wc_ref, o_ref):
    def normed(o_r, g_r, width):
        o = o_r[...]
        ms = jnp.sum(o * o, axis=-1, keepdims=True) * (1.0 / width)
        return (o * lax.rsqrt(ms + EPS) * g_r[...]).astype(BF16)

    ya = normed(oa_ref, ga_ref, H_FOX * HEAD_DIM)
    yb = normed(ob_ref, gb_ref, H_MOBA * HEAD_DIM)
    yc = normed(oc_ref, gc_ref, H_DIL * HEAD_DIM)
    o_ref[...] = x_ref[...] + _dot(ya, wa_ref[...]) + _dot(yb, wb_ref[...]) + _dot(yc, wc_ref[...])


def _outproj(x2, oa, ob, oc, ga, gb, gc, wa, wb, wc):
    T = x2.shape[0]
    row = lambda w: pl.BlockSpec((OUT_TM, w), lambda i: (i, 0))
    full = lambda a, b: pl.BlockSpec((a, b), lambda i: (0, 0))
    return pl.pallas_call(
        _outproj_kernel,
        grid=(T // OUT_TM,),
        in_specs=[row(D_MODEL), row(MIX_W), row(MIX_W), row(MIX_W),
                  full(1, MIX_W), full(1, MIX_W), full(1, MIX_W),
                  full(MIX_W, D_MODEL), full(MIX_W, D_MODEL), full(MIX_W, D_MODEL)],
        out_specs=row(D_MODEL),
        out_shape=jax.ShapeDtypeStruct((T, D_MODEL), F32),
        compiler_params=_cparams(("parallel",)),
        name="outproj",
    )(x2, oa, ob, oc, ga, gb, gc, wa, wb, wc)


FFN_TM = 2048
FFN_TF = 512
FFN_SUB = 256


def _swiglu_act(a, b):
    return a * (1.0 / (1.0 + jnp.exp(-a))) * b


def _ffn_kernel(x_ref, g_ref, w1_ref, w3_ref, w2_ref, o_ref, h_scr):
    f = pl.program_id(1)

    @pl.when(f == 0)
    def _():
        x = x_ref[...]
        h_scr[...] = _rms(x, g_ref[...]).astype(BF16)
        o_ref[...] = x

    for c in range(FFN_TM // FFN_SUB):
        rows = slice(c * FFN_SUB, (c + 1) * FFN_SUB)
        h = h_scr[rows, :]
        act = _swiglu_act(_dot(h, w1_ref[...]), _dot(h, w3_ref[...])).astype(BF16)
        o_ref[rows, :] = o_ref[rows, :] + _dot(act, w2_ref[...])


def _ffn(x2, g, w1, w3, w2):
    T = x2.shape[0]
    return pl.pallas_call(
        _ffn_kernel,
        grid=(T // FFN_TM, D_FF // FFN_TF),
        in_specs=[
            pl.BlockSpec((FFN_TM, D_MODEL), lambda i, f: (i, 0)),
            pl.BlockSpec((1, D_MODEL), lambda i, f: (0, 0)),
            pl.BlockSpec((D_MODEL, FFN_TF), lambda i, f: (0, f)),
            pl.BlockSpec((D_MODEL, FFN_TF), lambda i, f: (0, f)),
            pl.BlockSpec((FFN_TF, D_MODEL), lambda i, f: (f, 0)),
        ],
        out_specs=pl.BlockSpec((FFN_TM, D_MODEL), lambda i, f: (i, 0)),
        out_shape=jax.ShapeDtypeStruct((T, D_MODEL), F32),
        scratch_shapes=[pltpu.VMEM((FFN_TM, D_MODEL), BF16)],
        compiler_params=_cparams(("parallel", "arbitrary")),
        name="ffn_dense",
    )(x2, g, w1, w3, w2)


MOE_TT = 2048
MOE_CH = 256
MOE_UNIT = 128
MOE_TF = 512
MOE_KB = 256
MOE_NKB = MOE_TT // MOE_KB
MOE_GATHER_WIDTHS = (2, 6)


def _router_kernel(x_ref, g_ref, wr_hi_ref, wr_lo_ref, tri_ref,
                   h_ref, posc_ref, gatec_ref, posr_ref, cnt_ref):
    h = _rms(x_ref[...], g_ref[...])
    h_hi = h.astype(BF16)
    h_lo = (h - h_hi.astype(F32)).astype(BF16)
    h_ref[...] = h_hi
    wr_hi, wr_lo = wr_hi_ref[...], wr_lo_ref[...]
    logits = _dot(h_hi, wr_hi) + _dot(h_lo, wr_hi) + _dot(h_hi, wr_lo)
    lane = lax.broadcasted_iota(jnp.int32, (1, LANE), 1).astype(F32)
    lg = jnp.where(lane < N_EXPERTS, logits, -jnp.inf)
    m1 = jnp.max(lg, axis=1, keepdims=True)
    i1 = jnp.min(jnp.where(lg == m1, lane, float(LANE)), axis=1, keepdims=True)
    lg2 = jnp.where(lane == i1, -jnp.inf, lg)
    m2 = jnp.max(lg2, axis=1, keepdims=True)
    i2 = jnp.min(jnp.where(lg2 == m2, lane, float(LANE)), axis=1, keepdims=True)
    e2 = jnp.exp(m2 - m1)
    g1 = 1.0 / (1.0 + e2)
    g2 = e2 * g1
    routed = (lane == i1) | (lane == i2)
    gatec_ref[...] = jnp.where(lane == i1, g1, jnp.where(lane == i2, g2, 0.0))
    routed_f = jnp.where(routed, 1.0, 0.0)
    blocks = _cumsum_rows(routed_f, tri_ref[...], exact_small_ints=True)
    for b, cnt in enumerate(blocks):
        sl = slice(b * LANE, (b + 1) * LANE)
        pos = jnp.where(routed_f[sl] > 0.5, cnt - 1.0, -1.0)
        posc_ref[sl, :] = pos
        posr_ref[0, :, sl] = pos.T[:8, :]
    per_kb = MOE_KB // LANE
    for kb in range(MOE_NKB):
        cnt_ref[0, kb:kb + 1, :] = blocks[(kb + 1) * per_kb - 1][LANE - 1:LANE, :]


def _router(x2, g, wr_hi, wr_lo, tri):
    T = x2.shape[0]
    nt = T // MOE_TT
    return pl.pallas_call(
        _router_kernel,
        grid=(nt,),
        in_specs=[
            pl.BlockSpec((MOE_TT, D_MODEL), lambda i: (i, 0)),
            pl.BlockSpec((1, D_MODEL), lambda i: (0, 0)),
            pl.BlockSpec((D_MODEL, LANE), lambda i: (0, 0)),
            pl.BlockSpec((D_MODEL, LANE), lambda i: (0, 0)),
            pl.BlockSpec((LANE, LANE), lambda i: (0, 0)),
        ],
        out_specs=[
            pl.BlockSpec((MOE_TT, D_MODEL), lambda i: (i, 0)),
            pl.BlockSpec((MOE_TT, LANE), lambda i: (i, 0)),
            pl.BlockSpec((MOE_TT, LANE), lambda i: (i, 0)),
            pl.BlockSpec((1, 8, MOE_TT), lambda i: (i, 0, 0)),
            pl.BlockSpec((1, MOE_NKB, LANE), lambda i: (i, 0, 0)),
        ],
        out_shape=[
            jax.ShapeDtypeStruct((T, D_MODEL), BF16),
            jax.ShapeDtypeStruct((T, LANE), F32),
            jax.ShapeDtypeStruct((T, LANE), F32),
            jax.ShapeDtypeStruct((nt, 8, MOE_TT), F32),
            jax.ShapeDtypeStruct((nt, MOE_NKB, LANE), F32),
        ],
        compiler_params=_cparams(("parallel",)),
        name="moe_router",
    )(x2, g, wr_hi, wr_lo, tri)


def _moe_kernel(pref_ref, h_ref, posc_ref, gatec_ref, posr_ref, w1_ref, w3_ref, w2_ref, o_ref,
                xe_scr, ye_scr, pe_scr, ge_scr):
    t, e, f = pl.program_id(0), pl.program_id(1), pl.program_id(2)
    nf = pl.num_programs(2)
    base = (t * N_EXPERTS + e) * (MOE_NKB + 1)
    pref = [pref_ref[base + kb] for kb in range(MOE_NKB + 1)]
    n_rows = pref[MOE_NKB]
    lane = lax.broadcasted_iota(jnp.int32, (1, LANE), 1)

    @pl.when((e == 0) & (f == 0))
    def _():
        o_ref[...] = jnp.zeros((MOE_TT, D_MODEL), F32)

    @pl.when(f == 0)
    def _():
        is_e = lane == e
        pe_scr[...] = jnp.sum(jnp.where(is_e, posc_ref[...], 0.0), axis=1, keepdims=True)
        ge_scr[...] = jnp.sum(jnp.where(is_e, gatec_ref[...], 0.0), axis=1, keepdims=True)
        slot = lax.broadcasted_iota(jnp.int32, (MOE_CH, 1), 0).astype(F32)

        def gather(c, _):
            r0 = pl.multiple_of(c * MOE_CH, MOE_CH)
            rows = pl.ds(r0, MOE_CH)
            slots = slot + r0.astype(F32)
            ye_scr[rows, :] = jnp.zeros((MOE_CH, D_MODEL), F32)
            kb_lo = sum((pref[kb + 1] <= r0).astype(jnp.int32) for kb in range(MOE_NKB))
            kb_hi = MOE_NKB - 1 - sum((pref[kb] >= r0 + MOE_CH).astype(jnp.int32) for kb in range(MOE_NKB))
            n_kb = kb_hi - kb_lo + 1

            def one_dot(width):
                t0 = pl.multiple_of(jnp.minimum(kb_lo, MOE_NKB - width) * MOE_KB, MOE_KB)
                tok = pl.ds(t0, width * MOE_KB)
                onehot = jnp.where(posr_ref[0, pl.ds(e, 1), tok] == slots, 1.0, 0.0).astype(BF16)
                xe_scr[rows, :] = _dot(onehot, h_ref[tok, :]).astype(BF16)

            lo_w, mid_w = MOE_GATHER_WIDTHS
            pl.when(n_kb <= lo_w)(lambda: one_dot(lo_w))
            pl.when((n_kb > lo_w) & (n_kb <= mid_w))(lambda: one_dot(mid_w))
            pl.when(n_kb > mid_w)(lambda: one_dot(MOE_NKB))
            return 0

        lax.fori_loop(0, (n_rows + MOE_CH - 1) // MOE_CH, gather, 0)

    def expert_rows(r0, n):
        rows = pl.ds(r0, n)
        xc = xe_scr[rows, :]
        act = _swiglu_act(_dot(xc, w1_ref[0]), _dot(xc, w3_ref[0])).astype(BF16)
        ye_scr[rows, :] = ye_scr[rows, :] + _dot(act, w2_ref[0])

    n_units = (n_rows + MOE_UNIT - 1) // MOE_UNIT
    units_per_trip = 2 * MOE_CH // MOE_UNIT

    def chunks(r0, sizes):
        for n in sizes:
            expert_rows(r0, n)
            r0 = r0 + n

    def two_chunks(i, _):
        chunks(pl.multiple_of(i * (2 * MOE_CH), 2 * MOE_CH), (MOE_CH, MOE_CH))
        return 0

    n_trips = n_units // units_per_trip
    left = n_units - n_trips * units_per_trip
    ride = (left == 1) & (n_trips >= 1)
    n_plain = n_trips - ride.astype(jnp.int32)
    lax.fori_loop(0, n_plain, two_chunks, 0)
    tail0 = pl.multiple_of(n_plain * (2 * MOE_CH), 2 * MOE_CH)
    pl.when(ride)(lambda: chunks(tail0, (MOE_CH, MOE_CH, MOE_UNIT)))
    pl.when((left == 1) & (n_trips == 0))(lambda: chunks(tail0, (MOE_UNIT,)))
    pl.when(left == 2)(lambda: chunks(tail0, (MOE_CH,)))
    pl.when(left == 3)(lambda: chunks(tail0, (MOE_CH, MOE_UNIT)))

    @pl.when(f == nf - 1)
    def _():
        for kb in range(MOE_NKB):
            lo, hi = pref[kb], pref[kb + 1]
            tok = slice(kb * MOE_KB, (kb + 1) * MOE_KB)
            w_lo = lo // MOE_CH
            w_hi = (hi - 1) // MOE_CH

            def combine(n_win, w0, tok=tok):
                r0 = pl.multiple_of(w0 * MOE_CH, MOE_CH)
                slots = (lax.broadcasted_iota(jnp.int32, (1, n_win * MOE_CH), 1) + r0).astype(F32)
                onehot_t = jnp.where(pe_scr[tok, :] == slots, 1.0, 0.0).astype(BF16)
                y = ye_scr[pl.ds(r0, n_win * MOE_CH), :].astype(BF16)
                o_ref[tok, :] = o_ref[tok, :] + ge_scr[tok, :] * _dot(onehot_t, y)

            pl.when((hi > lo) & (w_hi == w_lo))(functools.partial(combine, 1, w_lo))
            pl.when((hi > lo) & (w_hi > w_lo))(functools.partial(combine, 2, w_lo))


def _moe(counts, h, posc, gatec, posr, we1, we3, we2):
    T = h.shape[0]
    nt = T // MOE_TT
    grid_spec = pltpu.PrefetchScalarGridSpec(
        num_scalar_prefetch=1,
        grid=(nt, N_EXPERTS, D_FF // MOE_TF),
        in_specs=[
            pl.BlockSpec((MOE_TT, D_MODEL), lambda t, e, f, c: (t, 0)),
            pl.BlockSpec((MOE_TT, LANE), lambda t, e, f, c: (t, 0)),
            pl.BlockSpec((MOE_TT, LANE), lambda t, e, f, c: (t, 0)),
            pl.BlockSpec((1, 8, MOE_TT), lambda t, e, f, c: (t, 0, 0)),
            pl.BlockSpec((1, D_MODEL, MOE_TF), lambda t, e, f, c: (e, 0, f)),
            pl.BlockSpec((1, D_MODEL, MOE_TF), lambda t, e, f, c: (e, 0, f)),
            pl.BlockSpec((1, MOE_TF, D_MODEL), lambda t, e, f, c: (e, f, 0)),
        ],
        out_specs=pl.BlockSpec((MOE_TT, D_MODEL), lambda t, e, f, c: (t, 0)),
        scratch_shapes=[
            pltpu.VMEM((MOE_TT, D_MODEL), BF16),
            pltpu.VMEM((MOE_TT, D_MODEL), F32),
            pltpu.VMEM((MOE_TT, 1), F32),
            pltpu.VMEM((MOE_TT, 1), F32),
        ],
    )
    return pl.pallas_call(
        _moe_kernel,
        grid_spec=grid_spec,
        out_shape=jax.ShapeDtypeStruct((T, D_MODEL), F32),
        compiler_params=_cparams(("parallel", "arbitrary", "arbitrary")),
        name="moe_experts",
    )(counts, h, posc, gatec, posr, we1, we3, we2)


FIN_TM = 1024


def _final_kernel(x_ref, y_ref, g_ref, o_ref):
    o_ref[...] = _rms(x_ref[...] + y_ref[...], g_ref[...])


def _final(x2, y2, g):
    T = x2.shape[0]
    row = pl.BlockSpec((FIN_TM, D_MODEL), lambda i: (i, 0))
    return pl.pallas_call(
        _final_kernel,
        grid=(T // FIN_TM,),
        in_specs=[row, row, pl.BlockSpec((1, D_MODEL), lambda i: (0, 0))],
        out_specs=row,
        out_shape=jax.ShapeDtypeStruct((T, D_MODEL), F32),
        compiler_params=_cparams(("parallel",)),
        name="final_norm",
    )(x2, y2, g)


def _rope_lane_tables():
    half = ROT_DIM // 2
    inv = ROPE_THETA ** (-jnp.arange(0, ROT_DIM, 2, dtype=F32) / ROT_DIM)
    ang = jnp.arange(SEQ, dtype=F32)[:, None] * inv[None, :]
    cos, sin = jnp.cos(ang), jnp.sin(ang)
    ones = jnp.ones((SEQ, HEAD_DIM - ROT_DIM), F32)
    zeros = jnp.zeros((SEQ, HEAD_DIM - ROT_DIM), F32)
    z8 = jnp.zeros((SEQ, half), F32)
    cos_h = jnp.concatenate([cos, cos, ones], axis=1)
    sp_h = jnp.concatenate([z8, sin, zeros], axis=1)
    sm_h = jnp.concatenate([-sin, z8, zeros], axis=1)
    two = lambda a: jnp.concatenate([a, a], axis=1)
    return two(cos_h), two(sp_h), two(sm_h)


def _pad_lanes(a, width=LANE):
    return jnp.pad(a, ((0, 0), (0, width - a.shape[1])))


def _mixer(x2, B, norm_mix, w_in, b_f, mix_gain, w_out, tables, tri, lm):
    w_qk = w_in[:, :QK_W].astype(BF16)
    w_vt = w_in[:, QK_W:QKV_W].T.astype(BF16)
    w_f = _pad_lanes(w_in[:, QKV_W:]).astype(BF16)
    qk, vt3, f_logit = _inproj(x2, norm_mix[None, :], w_qk, w_vt, w_f, *tables)
    cbc, crow = _forget_cumsum(f_logit.reshape(B, SEQ, LANE), _pad_lanes(b_f[None, :]), tri)
    qk3 = qk.reshape(B, SEQ, QK_W)
    oa = _attention("fox", 0, H_FOX, qk3, vt3, (cbc, crow))
    ob = _attention("moba", H_FOX, H_MOBA, qk3, vt3, ())
    oc = _attention("dil", H_FOX + H_MOBA, H_DIL, qk3, vt3, (lm,))
    a0, b0, c0 = 0, (H_FOX // 2) * PAIR, ((H_FOX + H_MOBA) // 2) * PAIR
    T = x2.shape[0]
    sl = lambda v, s: v[s:s + MIX_W]
    return _outproj(
        x2, oa.reshape(T, MIX_W), ob.reshape(T, MIX_W), oc.reshape(T, MIX_W),
        sl(mix_gain, a0)[None, :], sl(mix_gain, b0)[None, :], sl(mix_gain, c0)[None, :],
        sl(w_out, a0).astype(BF16), sl(w_out, b0).astype(BF16), sl(w_out, c0).astype(BF16))


def kernel(x, l0_norm_mix, l0_w_in, l0_b_f, l0_mix_gain, l0_w_out, l0_norm_ffn, l0_w1, l0_w3, l0_w2, l1_norm_mix, l1_w_in, l1_b_f, l1_mix_gain, l1_w_out, l1_norm_ffn, l1_w_router, l1_we1, l1_we3, l1_we2, final_norm):
    B, S, D = x.shape
    assert (S, D) == (SEQ, D_MODEL)
    T = B * S
    tables = _rope_lane_tables()
    tri = jnp.asarray(np.tril(np.ones((LANE, LANE), np.float32)), BF16)
    lm = jnp.asarray(_dilated_log2_multiplicity_t())

    x2 = x.reshape(T, D)
    x2 = _mixer(x2, B, l0_norm_mix, l0_w_in, l0_b_f, l0_mix_gain, l0_w_out, tables, tri, lm)
    x2 = _ffn(x2, l0_norm_ffn[None, :], l0_w1.astype(BF16), l0_w3.astype(BF16), l0_w2.astype(BF16))
    x2 = _mixer(x2, B, l1_norm_mix, l1_w_in, l1_b_f, l1_mix_gain, l1_w_out, tables, tri, lm)

    wr = _pad_lanes(l1_w_router)
    wr_hi = wr.astype(BF16)
    wr_lo = (wr - wr_hi.astype(F32)).astype(BF16)
    h, posc, gatec, posr, cnt = _router(x2, l1_norm_ffn[None, :], wr_hi, wr_lo, tri)
    ends = cnt[:, :, :N_EXPERTS].astype(jnp.int32).transpose(0, 2, 1)
    pref = jnp.pad(ends, ((0, 0), (0, 0), (1, 0))).reshape(-1)
    y2 = _moe(pref, h, posc, gatec, posr,
              l1_we1.astype(BF16), l1_we3.astype(BF16), l1_we2.astype(BF16))
    return _final(x2, y2, final_norm[None, :]).reshape(B, S, D)
```

```python
import functools
import math

import numpy as np
import jax
import jax.numpy as jnp
from jax import lax
from jax.experimental import pallas as pl
from jax.experimental.pallas import tpu as pltpu

D_MODEL = 1024
SEQ = 2048
HEAD_DIM = 64
N_HEADS = 16
H_FOX = 5
H_MOBA = 5
H_DIL = 6
ROT_DIM = 16
ROPE_THETA = 500000.0
MOBA_BLOCK = 256
MOBA_TOPK = 3
DIL_PATTERNS = ((128, 1), (512, 4), (2048, 16))
D_FF = 3584
N_EXPERTS = 8
EPS = 1e-6

LANE = 128
PAIR = 2 * HEAD_DIM
MIX = N_HEADS * HEAD_DIM
FOX_SLOTS = 2 * ((H_FOX + 1) // 2)
QK_W = 2 * MIX
QKV_W = 3 * MIX
ATT_BLK = 256
N_ATT_BLK = SEQ // ATT_BLK
NEG = -1e30
ONES_ROWS = 16
LOG2E = math.log2(math.e)
Q_SCALE = HEAD_DIM ** -0.5 * LOG2E
VMEM_LIMIT = 56 * 1024 * 1024

BF16 = jnp.bfloat16
F32 = jnp.float32


def _dot(a, b):
    return jnp.dot(a, b, preferred_element_type=F32)


def _dot_nt(a, b):
    return lax.dot_general(a, b, (((1,), (1,)), ((), ())), preferred_element_type=F32)


def _split3(x):
    hi = x.astype(BF16)
    r1 = x - hi.astype(F32)
    mid = r1.astype(BF16)
    lo = (r1 - mid.astype(F32)).astype(BF16)
    return hi, mid, lo


def _rms(x, g):
    return x * lax.rsqrt(jnp.mean(x * x, axis=-1, keepdims=True) + EPS) * g


def _cparams(sem):
    return pltpu.CompilerParams(dimension_semantics=sem, vmem_limit_bytes=VMEM_LIMIT)


IN_TM = 512
IN_TN = 512


def _inproj_kernel(x_ref, g_ref, w_ref, wvt_ref, wf_ref, cos_ref, sp_ref, sm_ref, qk_ref, vt_ref, f_ref):
    hb = _rms(x_ref[...], g_ref[...]).astype(BF16)
    cos, sp, sm = cos_ref[...], sp_ref[...], sm_ref[...]
    lane = lax.broadcasted_iota(jnp.int32, (1, LANE), 1)
    upper = lane >= HEAD_DIM
    n_pairs = N_HEADS // 2
    first_rot_head = H_FOX
    for c in range(QK_W // IN_TN):
        y = _dot(hb, w_ref[:, c * IN_TN:(c + 1) * IN_TN])
        for t in range(IN_TN // LANE):
            col = c * (IN_TN // LANE) + t
            part, pair = divmod(col, n_pairs)
            yt = y[:, t * LANE:(t + 1) * LANE]
            lo_head, hi_head = 2 * pair, 2 * pair + 1
            if hi_head >= first_rot_head:
                c_t, sp_t, sm_t = cos, sp, sm
                if lo_head < first_rot_head:
                    c_t = jnp.where(upper, cos, 1.0)
                    sp_t = jnp.where(upper, sp, 0.0)
                    sm_t = jnp.where(upper, sm, 0.0)
                half = ROT_DIM // 2
                yt = (yt * c_t + pltpu.roll(yt, half, axis=1) * sp_t
                      + pltpu.roll(yt, LANE - half, axis=1) * sm_t)
            if part == 0:
                yt = yt * Q_SCALE
            qk_ref[:, col * LANE:(col + 1) * LANE] = yt.astype(BF16)
    vt_ref[0] = _dot_nt(wvt_ref[...], hb).astype(BF16)
    f_ref[...] = _dot(hb, wf_ref[...])


def _inproj(x2, g, w_qk, w_vt, w_f, cos_t, sp_t, sm_t):
    T = x2.shape[0]
    n_pos = SEQ // IN_TM
    tab = pl.BlockSpec((IN_TM, LANE), lambda i: (i % n_pos, 0))
    return pl.pallas_call(
        _inproj_kernel,
        grid=(T // IN_TM,),
        in_specs=[
            pl.BlockSpec((IN_TM, D_MODEL), lambda i: (i, 0)),
            pl.BlockSpec((1, D_MODEL), lambda i: (0, 0)),
            pl.BlockSpec((D_MODEL, QK_W), lambda i: (0, 0)),
            pl.BlockSpec((MIX, D_MODEL), lambda i: (0, 0)),
            pl.BlockSpec((D_MODEL, LANE), lambda i: (0, 0)),
            tab, tab, tab,
        ],
        out_specs=[
            pl.BlockSpec((IN_TM, QK_W), lambda i: (i, 0)),
            pl.BlockSpec((1, MIX, IN_TM), lambda i: (i // n_pos, 0, i % n_pos)),
            pl.BlockSpec((IN_TM, LANE), lambda i: (i, 0)),
        ],
        out_shape=[
            jax.ShapeDtypeStruct((T, QK_W), BF16),
            jax.ShapeDtypeStruct((T // SEQ, MIX, SEQ), BF16),
            jax.ShapeDtypeStruct((T, LANE), F32),
        ],
        compiler_params=_cparams(("parallel",)),
        name="inproj",
    )(x2, g, w_qk, w_vt, w_f, cos_t, sp_t, sm_t)


def _cumsum_rows(x, tri, exact_small_ints):
    n = x.shape[0]
    carry = jnp.zeros((1, LANE), F32)
    out = []
    for b in range(n // LANE):
        xb = x[b * LANE:(b + 1) * LANE]
        if exact_small_ints:
            y = _dot(tri, xb.astype(BF16))
        else:
            hi, mid, lo = _split3(xb)
            y = _dot(tri, hi) + _dot(tri, mid) + _dot(tri, lo)
        y = y + carry
        carry = y[LANE - 1:LANE, :]
        out.append(y)
    return out


def _forget_kernel(f_ref, bf_ref, tri_ref, cbc_ref, crow_ref):
    z = f_ref[0] + bf_ref[...]
    logf = (jnp.minimum(z, 0.0) - jnp.log(1.0 + jnp.exp(-jnp.abs(z)))) * LOG2E
    blocks = _cumsum_rows(logf, tri_ref[...], exact_small_ints=False)
    for b, y in enumerate(blocks):
        yt = y.T
        for h in range(FOX_SLOTS):
            live = h < H_FOX
            cbc_ref[0, h, b * LANE:(b + 1) * LANE, :] = (
                jnp.broadcast_to(y[:, h:h + 1], (LANE, LANE)) if live else jnp.zeros((LANE, LANE), F32))
            crow_ref[0, h, :, b * LANE:(b + 1) * LANE] = yt[h:h + 1, :] if live else jnp.zeros((1, LANE), F32)


def _forget_cumsum(f3, bf_pad, tri):
    B = f3.shape[0]
    return pl.pallas_call(
        _forget_kernel,
        grid=(B,),
        in_specs=[
            pl.BlockSpec((1, SEQ, LANE), lambda b: (b, 0, 0)),
            pl.BlockSpec((1, LANE), lambda b: (0, 0)),
            pl.BlockSpec((LANE, LANE), lambda b: (0, 0)),
        ],
        out_specs=[
            pl.BlockSpec((1, FOX_SLOTS, SEQ, LANE), lambda b: (b, 0, 0, 0)),
            pl.BlockSpec((1, FOX_SLOTS, 1, SEQ), lambda b: (b, 0, 0, 0)),
        ],
        out_shape=[
            jax.ShapeDtypeStruct((B, FOX_SLOTS, SEQ, LANE), F32),
            jax.ShapeDtypeStruct((B, FOX_SLOTS, 1, SEQ), F32),
        ],
        compiler_params=_cparams(("parallel",)),
        name="forget_cumsum",
    )(f3, bf_pad, tri)


def _fold8(x, op):
    parts = [x[r:r + 8] for r in range(0, x.shape[0], 8)]
    while len(parts) > 1:
        parts = [op(parts[a], parts[a + 1]) for a in range(0, len(parts) - 1, 2)] + parts[len(parts) & ~1:]
    return parts[0]


def _attn_kernel(mode, h_first, h_last, q_ref, k_ref, vt_ref, *rest):
    if mode == "fox":
        cbc_ref, crow_ref, o_ref, s_scr = rest
    elif mode == "moba":
        o_ref, s_scr, km_scr = rest
    else:
        lmt_ref, o_ref, s_scr = rest
    pair = h_first // 2 + pl.program_id(1)
    lane = lax.broadcasted_iota(jnp.int32, (1, LANE), 1)
    key_idx = lax.broadcasted_iota(jnp.int32, (ATT_BLK, ATT_BLK), 0)
    qry_idx = lax.broadcasted_iota(jnp.int32, (ATT_BLK, ATT_BLK), 1)
    causal = key_idx <= qry_idx
    sub8 = lax.broadcasted_iota(jnp.int32, (8, ATT_BLK), 0)

    def block_means(slot, hmask):
        km_scr[slot] = jnp.zeros((LANE, LANE), F32)
        for n in range(N_ATT_BLK):
            kb = k_ref[0, n * ATT_BLK:(n + 1) * ATT_BLK, :].astype(F32)
            km = jnp.sum(kb, axis=0, keepdims=True) * (1.0 / MOBA_BLOCK)
            km_scr[slot, n:n + 1, :] = jnp.where(hmask, km, 0.0)
        return _split3(km_scr[slot])

    def query_block(slot, i, hmask, km3):
        r0 = i * ATT_BLK
        qm = jnp.where(hmask, q_ref[0, r0:r0 + ATT_BLK, :], 0.0).astype(BF16)
        if mode == "moba" and i > MOBA_TOPK:
            gt = (_dot_nt(km3[0], qm) + _dot_nt(km3[1], qm) + _dot_nt(km3[2], qm))[:8, :]
            rank = jnp.zeros((8, ATT_BLK), F32)
            for n2 in range(i):
                g2 = gt[n2:n2 + 1, :]
                ahead = (g2 > gt) | ((g2 == gt) & (n2 < sub8))
                rank = rank + jnp.where(ahead, 1.0, 0.0)
            sel_t = jnp.where((rank < MOBA_TOPK) & (sub8 < i), 1.0, 0.0)

        m8 = jnp.full((8, ATT_BLK), NEG, F32)
        for n in range(i + 1):
            c0 = n * ATT_BLK
            st = _dot_nt(k_ref[0, c0:c0 + ATT_BLK, :], qm)
            if mode == "fox":
                cs = cbc_ref[0, slot, c0:c0 + ATT_BLK, :]
                st = st - jnp.concatenate([cs, cs], axis=1)
            elif mode == "dil":
                st = st + lmt_ref[i - n]
            elif n < i and i > MOBA_TOPK:
                st = jnp.where(sel_t[n:n + 1, :] > 0.5, st, NEG)
            if n == i and mode != "dil":
                st = jnp.where(causal, st, NEG)
            s_scr[slot, n] = st
            m8 = jnp.maximum(m8, _fold8(st, jnp.maximum))
        m = jnp.max(m8, axis=0, keepdims=True)
        if mode == "fox":
            ct = crow_ref[0, slot, :, r0:r0 + ATT_BLK]
            shift = ct - (m + ct)
        else:
            shift = -m

        v_rows = slice(slot * HEAD_DIM, (slot + 1) * HEAD_DIM)
        if mode == "dil":
            acc_t = jnp.zeros((HEAD_DIM + ONES_ROWS, ATT_BLK), F32)
            ones = jnp.ones((ONES_ROWS, ATT_BLK), BF16)
            for n in range(i + 1):
                c0 = n * ATT_BLK
                p = jnp.exp2((s_scr[slot, n] + shift).astype(BF16))
                acc_t = acc_t + _dot(jnp.concatenate([vt_ref[0, v_rows, c0:c0 + ATT_BLK], ones], axis=0), p)
            return acc_t[:HEAD_DIM] * (1.0 / acc_t[HEAD_DIM:HEAD_DIM + 1])
        l8 = jnp.zeros((8, ATT_BLK), F32)
        acc_t = jnp.zeros((HEAD_DIM, ATT_BLK), F32)
        for n in range(i + 1):
            c0 = n * ATT_BLK
            p = jnp.exp2(s_scr[slot, n] + shift)
            l8 = l8 + _fold8(p, jnp.add)
            acc_t = acc_t + _dot(vt_ref[0, v_rows, c0:c0 + ATT_BLK], p.astype(BF16))
        return acc_t * (1.0 / jnp.sum(l8, axis=0, keepdims=True))

    def run(slots):
        hmasks = [(lane >= s * HEAD_DIM) & (lane < (s + 1) * HEAD_DIM) for s in (0, 1)]
        km3 = [block_means(s, hmasks[s]) if (mode == "moba" and s in slots) else None for s in (0, 1)]
        for i in range(N_ATT_BLK):
            parts = [query_block(s, i, hmasks[s], km3[s]) if s in slots
                     else jnp.zeros((HEAD_DIM, ATT_BLK), F32) for s in (0, 1)]
            o_ref[0, i * ATT_BLK:(i + 1) * ATT_BLK, :] = jnp.concatenate(parts, axis=0).T

    if h_first % 2 == 0 and h_last % 2 == 1:
        run((0, 1))
    else:
        both = (2 * pair >= h_first) & (2 * pair + 1 <= h_last)
        lone = (1,) if h_first % 2 == 1 else (0,)

        @pl.when(both)
        def _():
            run((0, 1))

        @pl.when(jnp.logical_not(both))
        def _():
            run(lone)


def _attention(mode, h_first, n_heads, qk3, vt3, extras):
    B = qk3.shape[0]
    n_pairs = N_HEADS // 2
    h_last = h_first + n_heads - 1
    pair0 = h_first // 2
    n_out_pairs = h_last // 2 - pair0 + 1

    def col(part):
        return lambda b, j: (b, 0, part * n_pairs + pair0 + j)

    in_specs = [pl.BlockSpec((1, SEQ, LANE), col(0)), pl.BlockSpec((1, SEQ, LANE), col(1)),
                pl.BlockSpec((1, LANE, SEQ), lambda b, j: (b, pair0 + j, 0))]
    scratch = [pltpu.VMEM((2, N_ATT_BLK, ATT_BLK, ATT_BLK), F32)]
    if mode == "fox":
        in_specs += [pl.BlockSpec((1, 2, SEQ, LANE), lambda b, j: (b, j, 0, 0)),
                     pl.BlockSpec((1, 2, 1, SEQ), lambda b, j: (b, j, 0, 0))]
    elif mode == "moba":
        scratch += [pltpu.VMEM((2, LANE, LANE), F32)]
    else:
        in_specs += [pl.BlockSpec((N_ATT_BLK, ATT_BLK, ATT_BLK), lambda b, j: (0, 0, 0))]
    return pl.pallas_call(
        functools.partial(_attn_kernel, mode, h_first, h_last),
        grid=(B, n_out_pairs),
        in_specs=in_specs,
        out_specs=pl.BlockSpec((1, SEQ, LANE), lambda b, j: (b, 0, j)),
        out_shape=jax.ShapeDtypeStruct((B, SEQ, n_out_pairs * LANE), F32),
        scratch_shapes=scratch,
        compiler_params=_cparams(("parallel", "parallel")),
        name="attn_" + mode,
    )(qk3, qk3, vt3, *extras)


def _dilated_log2_multiplicity_t():
    r = np.arange(ATT_BLK)[None, :]
    c = np.arange(ATT_BLK)[:, None]
    out = np.empty((N_ATT_BLK, ATT_BLK, ATT_BLK), np.float32)
    for d in range(N_ATT_BLK):
        dist = d * ATT_BLK + r - c
        mult = np.zeros_like(dist)
        for window, dil in DIL_PATTERNS:
            mult += ((dist >= 0) & (dist % dil == 0) & (dist // dil <= window // dil)).astype(dist.dtype)
        out[d] = np.where(mult > 0, np.log2(np.maximum(mult, 1)), NEG)
    return out


OUT_TM = 1024
MIX_W = 3 * LANE


def _outproj_kernel(x_ref, oa_ref, ob_ref, oc_ref, ga_ref, gb_ref, gc_ref, wa_ref, wb_ref, wc_ref, o_ref):
    def normed(o_r, g_r, width):
        o = o_r[...]
        ms = jnp.sum(o * o, axis=-1, keepdims=True) * (1.0 / width)
        return (o * lax.rsqrt(ms + EPS) * g_r[...]).astype(BF16)

    ya = normed(oa_ref, ga_ref, H_FOX * HEAD_DIM)
    yb = normed(ob_ref, gb_ref, H_MOBA * HEAD_DIM)
    yc = normed(oc_ref, gc_ref, H_DIL * HEAD_DIM)
    o_ref[...] = x_ref[...] + _dot(ya, wa_ref[...]) + _dot(yb, wb_ref[...]) + _dot(yc, wc_ref[...])


def _outproj(x2, oa, ob, oc, ga, gb, gc, wa, wb, wc):
    T = x2.shape[0]
    row = lambda w: pl.BlockSpec((OUT_TM, w), lambda i: (i, 0))
    full = lambda a, b: pl.BlockSpec((a, b), lambda i: (0, 0))
    return pl.pallas_call(
        _outproj_kernel,
        grid=(T // OUT_TM,),
        in_specs=[row(D_MODEL), row(MIX_W), row(MIX_W), row(MIX_W),
                  full(1, MIX_W), full(1, MIX_W), full(1, MIX_W),
                  full(MIX_W, D_MODEL), full(MIX_W, D_MODEL), full(MIX_W, D_MODEL)],
        out_specs=row(D_MODEL),
        out_shape=jax.ShapeDtypeStruct((T, D_MODEL), F32),
        compiler_params=_cparams(("parallel",)),
        name="outproj",
    )(x2, oa, ob, oc, ga, gb, gc, wa, wb, wc)


FFN_TM = 2048
FFN_TF = 512
FFN_SUB = 256


def _swiglu_act(a, b):
    return a * (1.0 / (1.0 + jnp.exp(-a))) * b


def _ffn_kernel(x_ref, g_ref, w1_ref, w3_ref, w2_ref, o_ref, h_scr):
    f = pl.program_id(1)

    @pl.when(f == 0)
    def _():
        x = x_ref[...]
        h_scr[...] = _rms(x, g_ref[...]).astype(BF16)
        o_ref[...] = x

    for c in range(FFN_TM // FFN_SUB):
        rows = slice(c * FFN_SUB, (c + 1) * FFN_SUB)
        h = h_scr[rows, :]
        act = _swiglu_act(_dot(h, w1_ref[...]), _dot(h, w3_ref[...])).astype(BF16)
        o_ref[rows, :] = o_ref[rows, :] + _dot(act, w2_ref[...])


def _ffn(x2, g, w1, w3, w2):
    T = x2.shape[0]
    return pl.pallas_call(
        _ffn_kernel,
        grid=(T // FFN_TM, D_FF // FFN_TF),
        in_specs=[
            pl.BlockSpec((FFN_TM, D_MODEL), lambda i, f: (i, 0)),
            pl.BlockSpec((1, D_MODEL), lambda i, f: (0, 0)),
            pl.BlockSpec((D_MODEL, FFN_TF), lambda i, f: (0, f)),
            pl.BlockSpec((D_MODEL, FFN_TF), lambda i, f: (0, f)),
            pl.BlockSpec((FFN_TF, D_MODEL), lambda i, f: (f, 0)),
        ],
        out_specs=pl.BlockSpec((FFN_TM, D_MODEL), lambda i, f: (i, 0)),
        out_shape=jax.ShapeDtypeStruct((T, D_MODEL), F32),
        scratch_shapes=[pltpu.VMEM((FFN_TM, D_MODEL), BF16)],
        compiler_params=_cparams(("parallel", "arbitrary")),
        name="ffn_dense",
    )(x2, g, w1, w3, w2)


MOE_TT = 2048
MOE_CH = 256
MOE_UNIT = 128
MOE_TF = 512
MOE_KB = 256
MOE_NKB = MOE_TT // MOE_KB
MOE_GATHER_WIDTHS = (2, 6)


def _router_kernel(x_ref, g_ref, wr_cat_ref, wr_hi_ref, tri_ref,
                   h_ref, posc_ref, gatec_ref, posr_ref, cnt_ref):
    h = _rms(x_ref[...], g_ref[...])
    h_hi = h.astype(BF16)
    h_lo = (h - h_hi.astype(F32)).astype(BF16)
    h_ref[...] = h_hi
    hh = _dot(h_hi, wr_cat_ref[...])
    logits = hh[:, :LANE] + hh[:, LANE:] + _dot(h_lo, wr_hi_ref[...])
    lane = lax.broadcasted_iota(jnp.int32, (1, LANE), 1).astype(F32)
    lg = jnp.where(lane < N_EXPERTS, logits, -jnp.inf)
    m1 = jnp.max(lg, axis=1, keepdims=True)
    i1 = jnp.min(jnp.where(lg == m1, lane, float(LANE)), axis=1, keepdims=True)
    lg2 = jnp.where(lane == i1, -jnp.inf, lg)
    m2 = jnp.max(lg2, axis=1, keepdims=True)
    i2 = jnp.min(jnp.where(lg2 == m2, lane, float(LANE)), axis=1, keepdims=True)
    e2 = jnp.exp(m2 - m1)
    g1 = 1.0 / (1.0 + e2)
    g2 = e2 * g1
    routed = (lane == i1) | (lane == i2)
    gatec_ref[...] = jnp.where(lane == i1, g1, jnp.where(lane == i2, g2, 0.0))
    routed_f = jnp.where(routed, 1.0, 0.0)
    blocks = _cumsum_rows(routed_f, tri_ref[...], exact_small_ints=True)
    for b, cnt in enumerate(blocks):
        sl = slice(b * LANE, (b + 1) * LANE)
        pos = jnp.where(routed_f[sl] > 0.5, cnt - 1.0, -1.0)
        posc_ref[sl, :] = pos
        posr_ref[0, :, sl] = pos.T[:8, :]
    per_kb = MOE_KB // LANE
    for kb in range(MOE_NKB):
        cnt_ref[0, kb:kb + 1, :] = blocks[(kb + 1) * per_kb - 1][LANE - 1:LANE, :]


def _router(x2, g, wr_cat, wr_hi, tri):
    T = x2.shape[0]
    nt = T // MOE_TT
    return pl.pallas_call(
        _router_kernel,
        grid=(nt,),
        in_specs=[
            pl.BlockSpec((MOE_TT, D_MODEL), lambda i: (i, 0)),
            pl.BlockSpec((1, D_MODEL), lambda i: (0, 0)),
            pl.BlockSpec((D_MODEL, 2 * LANE), lambda i: (0, 0)),
            pl.BlockSpec((D_MODEL, LANE), lambda i: (0, 0)),
            pl.BlockSpec((LANE, LANE), lambda i: (0, 0)),
        ],
        out_specs=[
            pl.BlockSpec((MOE_TT, D_MODEL), lambda i: (i, 0)),
            pl.BlockSpec((MOE_TT, LANE), lambda i: (i, 0)),
            pl.BlockSpec((MOE_TT, LANE), lambda i: (i, 0)),
            pl.BlockSpec((1, 8, MOE_TT), lambda i: (i, 0, 0)),
            pl.BlockSpec((1, MOE_NKB, LANE), lambda i: (i, 0, 0)),
        ],
        out_shape=[
            jax.ShapeDtypeStruct((T, D_MODEL), BF16),
            jax.ShapeDtypeStruct((T, LANE), F32),
            jax.ShapeDtypeStruct((T, LANE), F32),
            jax.ShapeDtypeStruct((nt, 8, MOE_TT), F32),
            jax.ShapeDtypeStruct((nt, MOE_NKB, LANE), F32),
        ],
        compiler_params=_cparams(("parallel",)),
        name="moe_router",
    )(x2, g, wr_cat, wr_hi, tri)


def _moe_kernel(pref_ref, h_ref, posc_ref, gatec_ref, posr_ref, w1_ref, w3_ref, w2_ref, o_ref,
                xe_scr, ye_scr, pe_scr, ge_scr):
    t, e, f = pl.program_id(0), pl.program_id(1), pl.program_id(2)
    nf = pl.num_programs(2)
    base = (t * N_EXPERTS + e) * (MOE_NKB + 1)
    pref = [pref_ref[base + kb] for kb in range(MOE_NKB + 1)]
    n_rows = pref[MOE_NKB]
    lane = lax.broadcasted_iota(jnp.int32, (1, LANE), 1)

    @pl.when((e == 0) & (f == 0))
    def _():
        o_ref[...] = jnp.zeros((MOE_TT, D_MODEL), F32)

    @pl.when(f == 0)
    def _():
        is_e = lane == e
        pe_scr[...] = jnp.sum(jnp.where(is_e, posc_ref[...], 0.0), axis=1, keepdims=True)
        ge_scr[...] = jnp.sum(jnp.where(is_e, gatec_ref[...], 0.0), axis=1, keepdims=True)
        slot = lax.broadcasted_iota(jnp.int32, (MOE_CH, 1), 0).astype(F32)

        def gather(c, _):
            r0 = pl.multiple_of(c * MOE_CH, MOE_CH)
            rows = pl.ds(r0, MOE_CH)
            slots = slot + r0.astype(F32)
            ye_scr[rows, :] = jnp.zeros((MOE_CH, D_MODEL), F32)
            kb_lo = sum((pref[kb + 1] <= r0).astype(jnp.int32) for kb in range(MOE_NKB))
            kb_hi = MOE_NKB - 1 - sum((pref[kb] >= r0 + MOE_CH).astype(jnp.int32) for kb in range(MOE_NKB))
            n_kb = kb_hi - kb_lo + 1

            def one_dot(width):
                t0 = pl.multiple_of(jnp.minimum(kb_lo, MOE_NKB - width) * MOE_KB, MOE_KB)
                tok = pl.ds(t0, width * MOE_KB)
                onehot = jnp.where(posr_ref[0, pl.ds(e, 1), tok] == slots, 1.0, 0.0).astype(BF16)
                xe_scr[rows, :] = _dot(onehot, h_ref[tok, :]).astype(BF16)

            lo_w, mid_w = MOE_GATHER_WIDTHS
            pl.when(n_kb <= lo_w)(lambda: one_dot(lo_w))
            pl.when((n_kb > lo_w) & (n_kb <= mid_w))(lambda: one_dot(mid_w))
            pl.when(n_kb > mid_w)(lambda: one_dot(MOE_NKB))
            return 0

        lax.fori_loop(0, (n_rows + MOE_CH - 1) // MOE_CH, gather, 0)

    def expert_rows(r0, n):
        rows = pl.ds(r0, n)
        xc = xe_scr[rows, :]
        act = _swiglu_act(_dot(xc, w1_ref[0]), _dot(xc, w3_ref[0])).astype(BF16)
        ye_scr[rows, :] = ye_scr[rows, :] + _dot(act, w2_ref[0])

    n_units = (n_rows + MOE_UNIT - 1) // MOE_UNIT
    units_per_trip = 2 * MOE_CH // MOE_UNIT

    def chunks(r0, sizes):
        for n in sizes:
            expert_rows(r0, n)
            r0 = r0 + n

    def two_chunks(i, _):
        chunks(pl.multiple_of(i * (2 * MOE_CH), 2 * MOE_CH), (MOE_CH, MOE_CH))
        return 0

    n_trips = n_units // units_per_trip
    left = n_units - n_trips * units_per_trip
    ride = (left == 1) & (n_trips >= 1)
    n_plain = n_trips - ride.astype(jnp.int32)
    lax.fori_loop(0, n_plain, two_chunks, 0)
    tail0 = pl.multiple_of(n_plain * (2 * MOE_CH), 2 * MOE_CH)
    pl.when(ride)(lambda: chunks(tail0, (MOE_CH, MOE_CH, MOE_UNIT)))
    pl.when((left == 1) & (n_trips == 0))(lambda: chunks(tail0, (MOE_UNIT,)))
    pl.when(left == 2)(lambda: chunks(tail0, (MOE_CH,)))
    pl.when(left == 3)(lambda: chunks(tail0, (MOE_CH, MOE_UNIT)))

    @pl.when(f == nf - 1)
    def _():
        for kb in range(MOE_NKB):
            lo, hi = pref[kb], pref[kb + 1]
            tok = slice(kb * MOE_KB, (kb + 1) * MOE_KB)
            w_lo = lo // MOE_CH
            w_hi = (hi - 1) // MOE_CH

            def combine(n_win, w0, tok=tok):
                r0 = pl.multiple_of(w0 * MOE_CH, MOE_CH)
                slots = (lax.broadcasted_iota(jnp.int32, (1, n_win * MOE_CH), 1) + r0).astype(F32)
                onehot_t = jnp.where(pe_scr[tok, :] == slots, 1.0, 0.0).astype(BF16)
                y = ye_scr[pl.ds(r0, n_win * MOE_CH), :].astype(BF16)
                o_ref[tok, :] = o_ref[tok, :] + ge_scr[tok, :] * _dot(onehot_t, y)

            pl.when((hi > lo) & (w_hi == w_lo))(functools.partial(combine, 1, w_lo))
            pl.when((hi > lo) & (w_hi > w_lo))(functools.partial(combine, 2, w_lo))


def _moe(counts, h, posc, gatec, posr, we1, we3, we2):
    T = h.shape[0]
    nt = T // MOE_TT
    grid_spec = pltpu.PrefetchScalarGridSpec(
        num_scalar_prefetch=1,
        grid=(nt, N_EXPERTS, D_FF // MOE_TF),
        in_specs=[
            pl.BlockSpec((MOE_TT, D_MODEL), lambda t, e, f, c: (t, 0)),
            pl.BlockSpec((MOE_TT, LANE), lambda t, e, f, c: (t, 0)),
            pl.BlockSpec((MOE_TT, LANE), lambda t, e, f, c: (t, 0)),
            pl.BlockSpec((1, 8, MOE_TT), lambda t, e, f, c: (t, 0, 0)),
            pl.BlockSpec((1, D_MODEL, MOE_TF), lambda t, e, f, c: (e, 0, f)),
            pl.BlockSpec((1, D_MODEL, MOE_TF), lambda t, e, f, c: (e, 0, f)),
            pl.BlockSpec((1, MOE_TF, D_MODEL), lambda t, e, f, c: (e, f, 0)),
        ],
        out_specs=pl.BlockSpec((MOE_TT, D_MODEL), lambda t, e, f, c: (t, 0)),
        scratch_shapes=[
            pltpu.VMEM((MOE_TT, D_MODEL), BF16),
            pltpu.VMEM((MOE_TT, D_MODEL), F32),
            pltpu.VMEM((MOE_TT, 1), F32),
            pltpu.VMEM((MOE_TT, 1), F32),
        ],
    )
    return pl.pallas_call(
        _moe_kernel,
        grid_spec=grid_spec,
        out_shape=jax.ShapeDtypeStruct((T, D_MODEL), F32),
        compiler_params=_cparams(("parallel", "arbitrary", "arbitrary")),
        name="moe_experts",
    )(counts, h, posc, gatec, posr, we1, we3, we2)


FIN_TM = 2048


def _final_kernel(x_ref, y_ref, g_ref, o_ref):
    o_ref[...] = _rms(x_ref[...] + y_ref[...], g_ref[...])


def _final(x2, y2, g):
    T = x2.shape[0]
    row = pl.BlockSpec((FIN_TM, D_MODEL), lambda i: (i, 0))
    return pl.pallas_call(
        _final_kernel,
        grid=(T // FIN_TM,),
        in_specs=[row, row, pl.BlockSpec((1, D_MODEL), lambda i: (0, 0))],
        out_specs=row,
        out_shape=jax.ShapeDtypeStruct((T, D_MODEL), F32),
        compiler_params=_cparams(("parallel",)),
        name="final_norm",
    )(x2, y2, g)


def _rope_lane_tables():
    half = ROT_DIM // 2
    inv = ROPE_THETA ** (-jnp.arange(0, ROT_DIM, 2, dtype=F32) / ROT_DIM)
    ang = jnp.arange(SEQ, dtype=F32)[:, None] * inv[None, :]
    cos, sin = jnp.cos(ang), jnp.sin(ang)
    ones = jnp.ones((SEQ, HEAD_DIM - ROT_DIM), F32)
    zeros = jnp.zeros((SEQ, HEAD_DIM - ROT_DIM), F32)
    z8 = jnp.zeros((SEQ, half), F32)
    cos_h = jnp.concatenate([cos, cos, ones], axis=1)
    sp_h = jnp.concatenate([z8, sin, zeros], axis=1)
    sm_h = jnp.concatenate([-sin, z8, zeros], axis=1)
    two = lambda a: jnp.concatenate([a, a], axis=1)
    return two(cos_h), two(sp_h), two(sm_h)


def _pad_lanes(a, width=LANE):
    return jnp.pad(a, ((0, 0), (0, width - a.shape[1])))


def _mixer(x2, B, norm_mix, w_in, b_f, mix_gain, w_out, tables, tri, lm):
    w_qk = w_in[:, :QK_W].astype(BF16)
    w_vt = w_in[:, QK_W:QKV_W].T.astype(BF16)
    w_f = _pad_lanes(w_in[:, QKV_W:]).astype(BF16)
    qk, vt3, f_logit = _inproj(x2, norm_mix[None, :], w_qk, w_vt, w_f, *tables)
    cbc, crow = _forget_cumsum(f_logit.reshape(B, SEQ, LANE), _pad_lanes(b_f[None, :]), tri)
    qk3 = qk.reshape(B, SEQ, QK_W)
    oa = _attention("fox", 0, H_FOX, qk3, vt3, (cbc, crow))
    ob = _attention("moba", H_FOX, H_MOBA, qk3, vt3, ())
    oc = _attention("dil", H_FOX + H_MOBA, H_DIL, qk3, vt3, (lm,))
    a0, b0, c0 = 0, (H_FOX // 2) * PAIR, ((H_FOX + H_MOBA) // 2) * PAIR
    T = x2.shape[0]
    sl = lambda v, s: v[s:s + MIX_W]
    return _outproj(
        x2, oa.reshape(T, MIX_W), ob.reshape(T, MIX_W), oc.reshape(T, MIX_W),
        sl(mix_gain, a0)[None, :], sl(mix_gain, b0)[None, :], sl(mix_gain, c0)[None, :],
        sl(w_out, a0).astype(BF16), sl(w_out, b0).astype(BF16), sl(w_out, c0).astype(BF16))


def kernel(x, l0_norm_mix, l0_w_in, l0_b_f, l0_mix_gain, l0_w_out, l0_norm_ffn, l0_w1, l0_w3, l0_w2, l1_norm_mix, l1_w_in, l1_b_f, l1_mix_gain, l1_w_out, l1_norm_ffn, l1_w_router, l1_we1, l1_we3, l1_we2, final_norm):
    B, S, D = x.shape
    assert (S, D) == (SEQ, D_MODEL)
    T = B * S
    tables = _rope_lane_tables()
    tri = jnp.asarray(np.tril(np.ones((LANE, LANE), np.float32)), BF16)
    lm = jnp.asarray(_dilated_log2_multiplicity_t())

    x2 = x.reshape(T, D)
    x2 = _mixer(x2, B, l0_norm_mix, l0_w_in, l0_b_f, l0_mix_gain, l0_w_out, tables, tri, lm)
    x2 = _ffn(x2, l0_norm_ffn[None, :], l0_w1.astype(BF16), l0_w3.astype(BF16), l0_w2.astype(BF16))
    x2 = _mixer(x2, B, l1_norm_mix, l1_w_in, l1_b_f, l1_mix_gain, l1_w_out, tables, tri, lm)

    wr = _pad_lanes(l1_w_router)
    wr_hi = wr.astype(BF16)
    wr_lo = (wr - wr_hi.astype(F32)).astype(BF16)
    wr_cat = jnp.concatenate([wr_hi, wr_lo], axis=1)
    h, posc, gatec, posr, cnt = _router(x2, l1_norm_ffn[None, :], wr_cat, wr_hi, tri)
    ends = cnt[:, :, :N_EXPERTS].astype(jnp.int32).transpose(0, 2, 1)
    pref = jnp.pad(ends, ((0, 0), (0, 0), (1, 0))).reshape(-1)
    y2 = _moe(pref, h, posc, gatec, posr,
              l1_we1.astype(BF16), l1_we3.astype(BF16), l1_we2.astype(BF16))
    return _final(x2, y2, final_norm[None, :]).reshape(B, S, D)
```

```python
import functools
import math

import numpy as np
import jax
import jax.numpy as jnp
from jax import lax
from jax.experimental import pallas as pl
from jax.experimental.pallas import tpu as pltpu

D_MODEL = 1024
SEQ = 2048
HEAD_DIM = 64
N_HEADS = 16
H_FOX = 5
H_MOBA = 5
H_DIL = 6
ROT_DIM = 16
ROPE_THETA = 500000.0
MOBA_BLOCK = 256
MOBA_TOPK = 3
DIL_PATTERNS = ((128, 1), (512, 4), (2048, 16))
D_FF = 3584
N_EXPERTS = 8
EPS = 1e-6

LANE = 128
PAIR = 2 * HEAD_DIM
MIX = N_HEADS * HEAD_DIM
FOX_SLOTS = 2 * ((H_FOX + 1) // 2)
QK_W = 2 * MIX
QKV_W = 3 * MIX
ATT_BLK = 256
N_ATT_BLK = SEQ // ATT_BLK
NEG = -1e30
ONES_ROWS = 16
LOG2E = math.log2(math.e)
Q_SCALE = HEAD_DIM ** -0.5 * LOG2E
VMEM_LIMIT = 56 * 1024 * 1024

BF16 = jnp.bfloat16
F32 = jnp.float32


def _dot(a, b):
    return jnp.dot(a, b, preferred_element_type=F32)


def _dot_nt(a, b):
    return lax.dot_general(a, b, (((1,), (1,)), ((), ())), preferred_element_type=F32)


def _split3(x):
    hi = x.astype(BF16)
    r1 = x - hi.astype(F32)
    mid = r1.astype(BF16)
    lo = (r1 - mid.astype(F32)).astype(BF16)
    return hi, mid, lo


def _rms(x, g):
    return x * lax.rsqrt(jnp.mean(x * x, axis=-1, keepdims=True) + EPS) * g


def _cparams(sem):
    return pltpu.CompilerParams(dimension_semantics=sem, vmem_limit_bytes=VMEM_LIMIT)


IN_TM = 512
IN_TN = 512


def _inproj_kernel(x_ref, g_ref, w_ref, wvt_ref, wf_ref, cos_ref, sp_ref, sm_ref, qk_ref, vt_ref, f_ref):
    hb = _rms(x_ref[...], g_ref[...]).astype(BF16)
    cos, sp, sm = cos_ref[...], sp_ref[...], sm_ref[...]
    lane = lax.broadcasted_iota(jnp.int32, (1, LANE), 1)
    upper = lane >= HEAD_DIM
    n_pairs = N_HEADS // 2
    first_rot_head = H_FOX
    for c in range(QK_W // IN_TN):
        y = _dot(hb, w_ref[:, c * IN_TN:(c + 1) * IN_TN])
        for t in range(IN_TN // LANE):
            col = c * (IN_TN // LANE) + t
            part, pair = divmod(col, n_pairs)
            yt = y[:, t * LANE:(t + 1) * LANE]
            lo_head, hi_head = 2 * pair, 2 * pair + 1
            if hi_head >= first_rot_head:
                c_t, sp_t, sm_t = cos, sp, sm
                if lo_head < first_rot_head:
                    c_t = jnp.where(upper, cos, 1.0)
                    sp_t = jnp.where(upper, sp, 0.0)
                    sm_t = jnp.where(upper, sm, 0.0)
                half = ROT_DIM // 2
                yt = (yt * c_t + pltpu.roll(yt, half, axis=1) * sp_t
                      + pltpu.roll(yt, LANE - half, axis=1) * sm_t)
            if part == 0:
                yt = yt * Q_SCALE
            qk_ref[:, col * LANE:(col + 1) * LANE] = yt.astype(BF16)
    vt_ref[0] = _dot_nt(wvt_ref[...], hb).astype(BF16)
    f_ref[...] = _dot(hb, wf_ref[...])


def _inproj(x2, g, w_qk, w_vt, w_f, cos_t, sp_t, sm_t):
    T = x2.shape[0]
    n_pos = SEQ // IN_TM
    tab = pl.BlockSpec((IN_TM, LANE), lambda i: (i % n_pos, 0))
    return pl.pallas_call(
        _inproj_kernel,
        grid=(T // IN_TM,),
        in_specs=[
            pl.BlockSpec((IN_TM, D_MODEL), lambda i: (i, 0)),
            pl.BlockSpec((1, D_MODEL), lambda i: (0, 0)),
            pl.BlockSpec((D_MODEL, QK_W), lambda i: (0, 0)),
            pl.BlockSpec((MIX, D_MODEL), lambda i: (0, 0)),
            pl.BlockSpec((D_MODEL, LANE), lambda i: (0, 0)),
            tab, tab, tab,
        ],
        out_specs=[
            pl.BlockSpec((IN_TM, QK_W), lambda i: (i, 0)),
            pl.BlockSpec((1, MIX, IN_TM), lambda i: (i // n_pos, 0, i % n_pos)),
            pl.BlockSpec((IN_TM, LANE), lambda i: (i, 0)),
        ],
        out_shape=[
            jax.ShapeDtypeStruct((T, QK_W), BF16),
            jax.ShapeDtypeStruct((T // SEQ, MIX, SEQ), BF16),
            jax.ShapeDtypeStruct((T, LANE), F32),
        ],
        compiler_params=_cparams(("parallel",)),
        name="inproj",
    )(x2, g, w_qk, w_vt, w_f, cos_t, sp_t, sm_t)


def _cumsum_rows(x, tri, exact_small_ints):
    n = x.shape[0]
    carry = jnp.zeros((1, LANE), F32)
    out = []
    for b in range(n // LANE):
        xb = x[b * LANE:(b + 1) * LANE]
        if exact_small_ints:
            y = _dot(tri, xb.astype(BF16))
        else:
            hi, mid, lo = _split3(xb)
            y = _dot(tri, hi) + _dot(tri, mid) + _dot(tri, lo)
        y = y + carry
        carry = y[LANE - 1:LANE, :]
        out.append(y)
    return out


def _forget_kernel(f_ref, bf_ref, tri_ref, cbc_ref, crow_ref):
    z = f_ref[0] + bf_ref[...]
    logf = (jnp.minimum(z, 0.0) - jnp.log(1.0 + jnp.exp(-jnp.abs(z)))) * LOG2E
    blocks = _cumsum_rows(logf, tri_ref[...], exact_small_ints=False)
    for b, y in enumerate(blocks):
        yt = y.T
        for h in range(FOX_SLOTS):
            live = h < H_FOX
            cbc_ref[0, h, b * LANE:(b + 1) * LANE, :] = (
                jnp.broadcast_to(y[:, h:h + 1], (LANE, LANE)) if live else jnp.zeros((LANE, LANE), F32))
            crow_ref[0, h, :, b * LANE:(b + 1) * LANE] = yt[h:h + 1, :] if live else jnp.zeros((1, LANE), F32)


def _forget_cumsum(f3, bf_pad, tri):
    B = f3.shape[0]
    return pl.pallas_call(
        _forget_kernel,
        grid=(B,),
        in_specs=[
            pl.BlockSpec((1, SEQ, LANE), lambda b: (b, 0, 0)),
            pl.BlockSpec((1, LANE), lambda b: (0, 0)),
            pl.BlockSpec((LANE, LANE), lambda b: (0, 0)),
        ],
        out_specs=[
            pl.BlockSpec((1, FOX_SLOTS, SEQ, LANE), lambda b: (b, 0, 0, 0)),
            pl.BlockSpec((1, FOX_SLOTS, 1, SEQ), lambda b: (b, 0, 0, 0)),
        ],
        out_shape=[
            jax.ShapeDtypeStruct((B, FOX_SLOTS, SEQ, LANE), F32),
            jax.ShapeDtypeStruct((B, FOX_SLOTS, 1, SEQ), F32),
        ],
        compiler_params=_cparams(("parallel",)),
        name="forget_cumsum",
    )(f3, bf_pad, tri)


def _fold8(x, op):
    parts = [x[r:r + 8] for r in range(0, x.shape[0], 8)]
    while len(parts) > 1:
        parts = [op(parts[a], parts[a + 1]) for a in range(0, len(parts) - 1, 2)] + parts[len(parts) & ~1:]
    return parts[0]


def _attn_kernel(mode, h_first, h_last, q_ref, k_ref, vt_ref, *rest):
    if mode == "fox":
        cbc_ref, crow_ref, o_ref, s_scr = rest
    elif mode == "moba":
        o_ref, s_scr, km_scr = rest
    else:
        lmt_ref, o_ref, s_scr = rest
    pair = h_first // 2 + pl.program_id(1)
    lane = lax.broadcasted_iota(jnp.int32, (1, LANE), 1)
    key_idx = lax.broadcasted_iota(jnp.int32, (ATT_BLK, ATT_BLK), 0)
    qry_idx = lax.broadcasted_iota(jnp.int32, (ATT_BLK, ATT_BLK), 1)
    causal = key_idx <= qry_idx
    sub8 = lax.broadcasted_iota(jnp.int32, (8, ATT_BLK), 0)

    def block_means(slot, hmask):
        km_scr[slot] = jnp.zeros((LANE, LANE), F32)
        for n in range(N_ATT_BLK):
            kb = k_ref[0, n * ATT_BLK:(n + 1) * ATT_BLK, :].astype(F32)
            km = jnp.sum(kb, axis=0, keepdims=True) * (1.0 / MOBA_BLOCK)
            km_scr[slot, n:n + 1, :] = jnp.where(hmask, km, 0.0)
        return _split3(km_scr[slot])

    def query_block(slot, i, hmask, km3):
        r0 = i * ATT_BLK
        qm = jnp.where(hmask, q_ref[0, r0:r0 + ATT_BLK, :], 0.0).astype(BF16)
        if mode == "moba" and i > MOBA_TOPK:
            gt = (_dot_nt(km3[0], qm) + _dot_nt(km3[1], qm) + _dot_nt(km3[2], qm))[:8, :]
            rank = jnp.zeros((8, ATT_BLK), F32)
            for n2 in range(i):
                g2 = gt[n2:n2 + 1, :]
                ahead = (g2 > gt) | ((g2 == gt) & (n2 < sub8))
                rank = rank + jnp.where(ahead, 1.0, 0.0)
            sel_t = jnp.where((rank < MOBA_TOPK) & (sub8 < i), 1.0, 0.0)

        m8 = jnp.full((8, ATT_BLK), NEG, F32)
        for n in range(i + 1):
            c0 = n * ATT_BLK
            st = _dot_nt(k_ref[0, c0:c0 + ATT_BLK, :], qm)
            if mode == "fox":
                cs = cbc_ref[0, slot, c0:c0 + ATT_BLK, :]
                st = st - jnp.concatenate([cs, cs], axis=1)
            elif mode == "dil":
                st = st + lmt_ref[i - n]
            elif n < i and i > MOBA_TOPK:
                st = jnp.where(sel_t[n:n + 1, :] > 0.5, st, NEG)
            if n == i and mode != "dil":
                st = jnp.where(causal, st, NEG)
            s_scr[slot, n] = st
            m8 = jnp.maximum(m8, _fold8(st, jnp.maximum))
        m = jnp.max(m8, axis=0, keepdims=True)
        if mode == "fox":
            ct = crow_ref[0, slot, :, r0:r0 + ATT_BLK]
            shift = ct - (m + ct)
        else:
            shift = -m

        v_rows = slice(slot * HEAD_DIM, (slot + 1) * HEAD_DIM)
        if mode == "dil":
            acc_t = jnp.zeros((HEAD_DIM + ONES_ROWS, ATT_BLK), F32)
            ones = jnp.ones((ONES_ROWS, ATT_BLK), BF16)
            for n in range(i + 1):
                c0 = n * ATT_BLK
                p = jnp.exp2((s_scr[slot, n] + shift).astype(BF16))
                acc_t = acc_t + _dot(jnp.concatenate([vt_ref[0, v_rows, c0:c0 + ATT_BLK], ones], axis=0), p)
            return acc_t[:HEAD_DIM] * (1.0 / acc_t[HEAD_DIM:HEAD_DIM + 1])
        l8 = jnp.zeros((8, ATT_BLK), F32)
        acc_t = jnp.zeros((HEAD_DIM, ATT_BLK), F32)
        for n in range(i + 1):
            c0 = n * ATT_BLK
            p = jnp.exp2(s_scr[slot, n] + shift)
            l8 = l8 + _fold8(p, jnp.add)
            acc_t = acc_t + _dot(vt_ref[0, v_rows, c0:c0 + ATT_BLK], p.astype(BF16))
        return acc_t * (1.0 / jnp.sum(l8, axis=0, keepdims=True))

    def run(slots):
        hmasks = [(lane >= s * HEAD_DIM) & (lane < (s + 1) * HEAD_DIM) for s in (0, 1)]
        km3 = [block_means(s, hmasks[s]) if (mode == "moba" and s in slots) else None for s in (0, 1)]
        for i in range(N_ATT_BLK):
            parts = [query_block(s, i, hmasks[s], km3[s]) if s in slots
                     else jnp.zeros((HEAD_DIM, ATT_BLK), F32) for s in (0, 1)]
            o_ref[0, i * ATT_BLK:(i + 1) * ATT_BLK, :] = jnp.concatenate(parts, axis=0).T

    if h_first % 2 == 0 and h_last % 2 == 1:
        run((0, 1))
    else:
        both = (2 * pair >= h_first) & (2 * pair + 1 <= h_last)
        lone = (1,) if h_first % 2 == 1 else (0,)

        @pl.when(both)
        def _():
            run((0, 1))

        @pl.when(jnp.logical_not(both))
        def _():
            run(lone)


def _attention(mode, h_first, n_heads, qk3, vt3, extras):
    B = qk3.shape[0]
    n_pairs = N_HEADS // 2
    h_last = h_first + n_heads - 1
    pair0 = h_first // 2
    n_out_pairs = h_last // 2 - pair0 + 1

    def col(part):
        return lambda b, j: (b, 0, part * n_pairs + pair0 + j)

    in_specs = [pl.BlockSpec((1, SEQ, LANE), col(0)), pl.BlockSpec((1, SEQ, LANE), col(1)),
                pl.BlockSpec((1, LANE, SEQ), lambda b, j: (b, pair0 + j, 0))]
    scratch = [pltpu.VMEM((2, N_ATT_BLK, ATT_BLK, ATT_BLK), F32)]
    if mode == "fox":
        in_specs += [pl.BlockSpec((1, 2, SEQ, LANE), lambda b, j: (b, j, 0, 0)),
                     pl.BlockSpec((1, 2, 1, SEQ), lambda b, j: (b, j, 0, 0))]
    elif mode == "moba":
        scratch += [pltpu.VMEM((2, LANE, LANE), F32)]
    else:
        in_specs += [pl.BlockSpec((N_ATT_BLK, ATT_BLK, ATT_BLK), lambda b, j: (0, 0, 0))]
    return pl.pallas_call(
        functools.partial(_attn_kernel, mode, h_first, h_last),
        grid=(B, n_out_pairs),
        in_specs=in_specs,
        out_specs=pl.BlockSpec((1, SEQ, LANE), lambda b, j: (b, 0, j)),
        out_shape=jax.ShapeDtypeStruct((B, SEQ, n_out_pairs * LANE), F32),
        scratch_shapes=scratch,
        compiler_params=_cparams(("parallel", "parallel")),
        name="attn_" + mode,
    )(qk3, qk3, vt3, *extras)


def _dilated_log2_multiplicity_t():
    r = np.arange(ATT_BLK)[None, :]
    c = np.arange(ATT_BLK)[:, None]
    out = np.empty((N_ATT_BLK, ATT_BLK, ATT_BLK), np.float32)
    for d in range(N_ATT_BLK):
        dist = d * ATT_BLK + r - c
        mult = np.zeros_like(dist)
        for window, dil in DIL_PATTERNS:
            mult += ((dist >= 0) & (dist % dil == 0) & (dist // dil <= window // dil)).astype(dist.dtype)
        out[d] = np.where(mult > 0, np.log2(np.maximum(mult, 1)), NEG)
    return out


OUT_TM = 1024
MIX_W = 3 * LANE


def _outproj_kernel(x_ref, oa_ref, ob_ref, oc_ref, ga_ref, gb_ref, gc_ref, wa_ref, wb_ref, wc_ref, o_ref):
    def normed(o_r, g_r, width):
        o = o_r[...]
        ms = jnp.sum(o * o, axis=-1, keepdims=True) * (1.0 / width)
        return (o * lax.rsqrt(ms + EPS) * g_r[...]).astype(BF16)

    ya = normed(oa_ref, ga_ref, H_FOX * HEAD_DIM)
    yb = normed(ob_ref, gb_ref, H_MOBA * HEAD_DIM)
    yc = normed(oc_ref, gc_ref, H_DIL * HEAD_DIM)
    o_ref[...] = x_ref[...] + _dot(ya, wa_ref[...]) + _dot(yb, wb_ref[...]) + _dot(yc, wc_ref[...])


def _outproj(x2, oa, ob, oc, ga, gb, gc, wa, wb, wc):
    T = x2.shape[0]
    row = lambda w: pl.BlockSpec((OUT_TM, w), lambda i: (i, 0))
    full = lambda a, b: pl.BlockSpec((a, b), lambda i: (0, 0))
    return pl.pallas_call(
        _outproj_kernel,
        grid=(T // OUT_TM,),
        in_specs=[row(D_MODEL), row(MIX_W), row(MIX_W), row(MIX_W),
                  full(1, MIX_W), full(1, MIX_W), full(1, MIX_W),
                  full(MIX_W, D_MODEL), full(MIX_W, D_MODEL), full(MIX_W, D_MODEL)],
        out_specs=row(D_MODEL),
        out_shape=jax.ShapeDtypeStruct((T, D_MODEL), F32),
        compiler_params=_cparams(("parallel",)),
        name="outproj",
    )(x2, oa, ob, oc, ga, gb, gc, wa, wb, wc)


FFN_TM = 2048
FFN_TF = 512
FFN_SUB = 256


def _swiglu_act(a, b):
    return a * (1.0 / (1.0 + jnp.exp(-a))) * b


def _ffn_kernel(x_ref, g_ref, w1_ref, w3_ref, w2_ref, o_ref, h_scr):
    f = pl.program_id(1)

    @pl.when(f == 0)
    def _():
        x = x_ref[...]
        h_scr[...] = _rms(x, g_ref[...]).astype(BF16)
        o_ref[...] = x

    for c in range(FFN_TM // FFN_SUB):
        rows = slice(c * FFN_SUB, (c + 1) * FFN_SUB)
        h = h_scr[rows, :]
        act = _swiglu_act(_dot(h, w1_ref[...]), _dot(h, w3_ref[...])).astype(BF16)
        o_ref[rows, :] = o_ref[rows, :] + _dot(act, w2_ref[...])


def _ffn(x2, g, w1, w3, w2):
    T = x2.shape[0]
    return pl.pallas_call(
        _ffn_kernel,
        grid=(T // FFN_TM, D_FF // FFN_TF),
        in_specs=[
            pl.BlockSpec((FFN_TM, D_MODEL), lambda i, f: (i, 0)),
            pl.BlockSpec((1, D_MODEL), lambda i, f: (0, 0)),
            pl.BlockSpec((D_MODEL, FFN_TF), lambda i, f: (0, f)),
            pl.BlockSpec((D_MODEL, FFN_TF), lambda i, f: (0, f)),
            pl.BlockSpec((FFN_TF, D_MODEL), lambda i, f: (f, 0)),
        ],
        out_specs=pl.BlockSpec((FFN_TM, D_MODEL), lambda i, f: (i, 0)),
        out_shape=jax.ShapeDtypeStruct((T, D_MODEL), F32),
        scratch_shapes=[pltpu.VMEM((FFN_TM, D_MODEL), BF16)],
        compiler_params=_cparams(("parallel", "arbitrary")),
        name="ffn_dense",
    )(x2, g, w1, w3, w2)


MOE_TT = 2048
MOE_CH = 256
MOE_UNIT = 64
MOE_TF = 512
MOE_KB = 256
MOE_NKB = MOE_TT // MOE_KB
MOE_GATHER_WIDTHS = (2, 6)


def _router_kernel(x_ref, g_ref, wr_cat_ref, wr_hi_ref, tri_ref,
                   h_ref, posc_ref, gatec_ref, posr_ref, cnt_ref):
    h = _rms(x_ref[...], g_ref[...])
    h_hi = h.astype(BF16)
    h_lo = (h - h_hi.astype(F32)).astype(BF16)
    h_ref[...] = h_hi
    hh = _dot(h_hi, wr_cat_ref[...])
    logits = hh[:, :LANE] + hh[:, LANE:] + _dot(h_lo, wr_hi_ref[...])
    lane = lax.broadcasted_iota(jnp.int32, (1, LANE), 1).astype(F32)
    lg = jnp.where(lane < N_EXPERTS, logits, -jnp.inf)
    m1 = jnp.max(lg, axis=1, keepdims=True)
    i1 = jnp.min(jnp.where(lg == m1, lane, float(LANE)), axis=1, keepdims=True)
    lg2 = jnp.where(lane == i1, -jnp.inf, lg)
    m2 = jnp.max(lg2, axis=1, keepdims=True)
    i2 = jnp.min(jnp.where(lg2 == m2, lane, float(LANE)), axis=1, keepdims=True)
    e2 = jnp.exp(m2 - m1)
    g1 = 1.0 / (1.0 + e2)
    g2 = e2 * g1
    routed = (lane == i1) | (lane == i2)
    gatec_ref[...] = jnp.where(lane == i1, g1, jnp.where(lane == i2, g2, 0.0))
    routed_f = jnp.where(routed, 1.0, 0.0)
    blocks = _cumsum_rows(routed_f, tri_ref[...], exact_small_ints=True)
    for b, cnt in enumerate(blocks):
        sl = slice(b * LANE, (b + 1) * LANE)
        pos = jnp.where(routed_f[sl] > 0.5, cnt - 1.0, -1.0)
        posc_ref[sl, :] = pos
        posr_ref[0, :, sl] = pos.T[:8, :]
    per_kb = MOE_KB // LANE
    for kb in range(MOE_NKB):
        cnt_ref[0, kb:kb + 1, :] = blocks[(kb + 1) * per_kb - 1][LANE - 1:LANE, :]


def _router(x2, g, wr_cat, wr_hi, tri):
    T = x2.shape[0]
    nt = T // MOE_TT
    return pl.pallas_call(
        _router_kernel,
        grid=(nt,),
        in_specs=[
            pl.BlockSpec((MOE_TT, D_MODEL), lambda i: (i, 0)),
            pl.BlockSpec((1, D_MODEL), lambda i: (0, 0)),
            pl.BlockSpec((D_MODEL, 2 * LANE), lambda i: (0, 0)),
            pl.BlockSpec((D_MODEL, LANE), lambda i: (0, 0)),
            pl.BlockSpec((LANE, LANE), lambda i: (0, 0)),
        ],
        out_specs=[
            pl.BlockSpec((MOE_TT, D_MODEL), lambda i: (i, 0)),
            pl.BlockSpec((MOE_TT, LANE), lambda i: (i, 0)),
            pl.BlockSpec((MOE_TT, LANE), lambda i: (i, 0)),
            pl.BlockSpec((1, 8, MOE_TT), lambda i: (i, 0, 0)),
            pl.BlockSpec((1, MOE_NKB, LANE), lambda i: (i, 0, 0)),
        ],
        out_shape=[
            jax.ShapeDtypeStruct((T, D_MODEL), BF16),
            jax.ShapeDtypeStruct((T, LANE), F32),
            jax.ShapeDtypeStruct((T, LANE), F32),
            jax.ShapeDtypeStruct((nt, 8, MOE_TT), F32),
            jax.ShapeDtypeStruct((nt, MOE_NKB, LANE), F32),
        ],
        compiler_params=_cparams(("parallel",)),
        name="moe_router",
    )(x2, g, wr_cat, wr_hi, tri)


def _moe_kernel(pref_ref, h_ref, posc_ref, gatec_ref, posr_ref, w1_ref, w3_ref, w2_ref, x_hbm, gfin_ref, o_ref,
                xe_scr, ye_scr, pe_scr, ge_scr, x_buf, x_sem):
    t, e, f = pl.program_id(0), pl.program_id(1), pl.program_id(2)
    nf = pl.num_programs(2)
    base = (t * N_EXPERTS + e) * (MOE_NKB + 1)
    pref = [pref_ref[base + kb] for kb in range(MOE_NKB + 1)]
    n_rows = pref[MOE_NKB]
    lane = lax.broadcasted_iota(jnp.int32, (1, LANE), 1)

    @pl.when((e == 0) & (f == 0))
    def _():
        o_ref[...] = jnp.zeros((MOE_TT, D_MODEL), F32)

    @pl.when(f == 0)
    def _():
        is_e = lane == e
        pe_scr[...] = jnp.sum(jnp.where(is_e, posc_ref[...], 0.0), axis=1, keepdims=True)
        ge_scr[...] = jnp.sum(jnp.where(is_e, gatec_ref[...], 0.0), axis=1, keepdims=True)
        slot = lax.broadcasted_iota(jnp.int32, (MOE_CH, 1), 0).astype(F32)

        def gather(c, _):
            r0 = pl.multiple_of(c * MOE_CH, MOE_CH)
            rows = pl.ds(r0, MOE_CH)
            slots = slot + r0.astype(F32)
            ye_scr[rows, :] = jnp.zeros((MOE_CH, D_MODEL), F32)
            kb_lo = sum((pref[kb + 1] <= r0).astype(jnp.int32) for kb in range(MOE_NKB))
            kb_hi = MOE_NKB - 1 - sum((pref[kb] >= r0 + MOE_CH).astype(jnp.int32) for kb in range(MOE_NKB))
            n_kb = kb_hi - kb_lo + 1

            def one_dot(width):
                t0 = pl.multiple_of(jnp.minimum(kb_lo, MOE_NKB - width) * MOE_KB, MOE_KB)
                tok = pl.ds(t0, width * MOE_KB)
                onehot = jnp.where(posr_ref[0, pl.ds(e, 1), tok] == slots, 1.0, 0.0).astype(BF16)
                xe_scr[rows, :] = _dot(onehot, h_ref[tok, :]).astype(BF16)

            lo_w, mid_w = MOE_GATHER_WIDTHS
            pl.when(n_kb <= lo_w)(lambda: one_dot(lo_w))
            pl.when((n_kb > lo_w) & (n_kb <= mid_w))(lambda: one_dot(mid_w))
            pl.when(n_kb > mid_w)(lambda: one_dot(MOE_NKB))
            return 0

        lax.fori_loop(0, (n_rows + MOE_CH - 1) // MOE_CH, gather, 0)

    def expert_rows(r0, n):
        rows = pl.ds(r0, n)
        xc = xe_scr[rows, :]
        act = _swiglu_act(_dot(xc, w1_ref[0]), _dot(xc, w3_ref[0])).astype(BF16)
        ye_scr[rows, :] = ye_scr[rows, :] + _dot(act, w2_ref[0])

    n_units = (n_rows + MOE_UNIT - 1) // MOE_UNIT
    units_per_trip = 2 * MOE_CH // MOE_UNIT
    half = MOE_CH // 2

    def chunks(r0, sizes):
        for n in sizes:
            expert_rows(r0, n)
            r0 = r0 + n

    def two_chunks(i, _):
        chunks(pl.multiple_of(i * (2 * MOE_CH), 2 * MOE_CH), (MOE_CH, MOE_CH))
        return 0

    n_trips = n_units // units_per_trip
    left = n_units - n_trips * units_per_trip
    ride = (left >= 1) & (left <= 2) & (n_trips >= 1)
    n_plain = n_trips - ride.astype(jnp.int32)
    lax.fori_loop(0, n_plain, two_chunks, 0)
    tail0 = pl.multiple_of(n_plain * (2 * MOE_CH), 2 * MOE_CH)
    pl.when(ride & (left == 1))(lambda: chunks(tail0, (MOE_CH, MOE_CH + MOE_UNIT)))
    pl.when(ride & (left == 2))(lambda: chunks(tail0, (MOE_CH, MOE_CH + 2 * MOE_UNIT)))
    tail_halves = jnp.where(ride, 0, (left * MOE_UNIT + half - 1) // half)
    pl.when(tail_halves == 1)(lambda: chunks(tail0, (half,)))
    pl.when(tail_halves == 2)(lambda: chunks(tail0, (MOE_CH,)))
    pl.when(tail_halves == 3)(lambda: chunks(tail0, (MOE_CH, half)))
    pl.when(tail_halves == 4)(lambda: chunks(tail0, (MOE_CH, MOE_CH)))

    @pl.when(f == nf - 1)
    def _():
        for kb in range(MOE_NKB):
            lo, hi = pref[kb], pref[kb + 1]
            tok = slice(kb * MOE_KB, (kb + 1) * MOE_KB)
            w_lo = lo // MOE_CH
            w_hi = (hi - 1) // MOE_CH

            def combine(n_win, w0, tok=tok):
                r0 = pl.multiple_of(w0 * MOE_CH, MOE_CH)
                slots = (lax.broadcasted_iota(jnp.int32, (1, n_win * MOE_CH), 1) + r0).astype(F32)
                onehot_t = jnp.where(pe_scr[tok, :] == slots, 1.0, 0.0).astype(BF16)
                y = ye_scr[pl.ds(r0, n_win * MOE_CH), :].astype(BF16)
                o_ref[tok, :] = o_ref[tok, :] + ge_scr[tok, :] * _dot(onehot_t, y)

            pl.when((hi > lo) & (w_hi == w_lo))(functools.partial(combine, 1, w_lo))
            pl.when((hi > lo) & (w_hi > w_lo))(functools.partial(combine, 2, w_lo))

    @pl.when((f == nf - 1) & (e == N_EXPERTS - 1))
    def _():
        def x_copy(kb):
            src = x_hbm.at[pl.ds(pl.multiple_of(t * MOE_TT + kb * MOE_KB, MOE_KB), MOE_KB), :]
            return pltpu.make_async_copy(src, x_buf.at[kb % 2], x_sem.at[kb % 2])

        x_copy(0).start()
        for kb in range(MOE_NKB):
            if kb + 1 < MOE_NKB:
                x_copy(kb + 1).start()
            x_copy(kb).wait()
            tok = slice(kb * MOE_KB, (kb + 1) * MOE_KB)
            o_ref[tok, :] = _rms(x_buf[kb % 2] + o_ref[tok, :], gfin_ref[...])


def _moe(counts, h, posc, gatec, posr, we1, we3, we2, x2, g_final):
    T = h.shape[0]
    nt = T // MOE_TT
    grid_spec = pltpu.PrefetchScalarGridSpec(
        num_scalar_prefetch=1,
        grid=(nt, N_EXPERTS, D_FF // MOE_TF),
        in_specs=[
            pl.BlockSpec((MOE_TT, D_MODEL), lambda t, e, f, c: (t, 0)),
            pl.BlockSpec((MOE_TT, LANE), lambda t, e, f, c: (t, 0)),
            pl.BlockSpec((MOE_TT, LANE), lambda t, e, f, c: (t, 0)),
            pl.BlockSpec((1, 8, MOE_TT), lambda t, e, f, c: (t, 0, 0)),
            pl.BlockSpec((1, D_MODEL, MOE_TF), lambda t, e, f, c: (e, 0, f)),
            pl.BlockSpec((1, D_MODEL, MOE_TF), lambda t, e, f, c: (e, 0, f)),
            pl.BlockSpec((1, MOE_TF, D_MODEL), lambda t, e, f, c: (e, f, 0)),
            pl.BlockSpec(memory_space=pl.ANY),
            pl.BlockSpec((1, D_MODEL), lambda t, e, f, c: (0, 0)),
        ],
        out_specs=pl.BlockSpec((MOE_TT, D_MODEL), lambda t, e, f, c: (t, 0)),
        scratch_shapes=[
            pltpu.VMEM((MOE_TT, D_MODEL), BF16),
            pltpu.VMEM((MOE_TT, D_MODEL), F32),
            pltpu.VMEM((MOE_TT, 1), F32),
            pltpu.VMEM((MOE_TT, 1), F32),
            pltpu.VMEM((2, MOE_KB, D_MODEL), F32),
            pltpu.SemaphoreType.DMA((2,)),
        ],
    )
    return pl.pallas_call(
        _moe_kernel,
        grid_spec=grid_spec,
        out_shape=jax.ShapeDtypeStruct((T, D_MODEL), F32),
        compiler_params=_cparams(("parallel", "arbitrary", "arbitrary")),
        name="moe_experts",
    )(counts, h, posc, gatec, posr, we1, we3, we2, x2, g_final)


def _rope_lane_tables():
    half = ROT_DIM // 2
    inv = ROPE_THETA ** (-jnp.arange(0, ROT_DIM, 2, dtype=F32) / ROT_DIM)
    ang = jnp.arange(SEQ, dtype=F32)[:, None] * inv[None, :]
    cos, sin = jnp.cos(ang), jnp.sin(ang)
    ones = jnp.ones((SEQ, HEAD_DIM - ROT_DIM), F32)
    zeros = jnp.zeros((SEQ, HEAD_DIM - ROT_DIM), F32)
    z8 = jnp.zeros((SEQ, half), F32)
    cos_h = jnp.concatenate([cos, cos, ones], axis=1)
    sp_h = jnp.concatenate([z8, sin, zeros], axis=1)
    sm_h = jnp.concatenate([-sin, z8, zeros], axis=1)
    two = lambda a: jnp.concatenate([a, a], axis=1)
    return two(cos_h), two(sp_h), two(sm_h)


def _pad_lanes(a, width=LANE):
    return jnp.pad(a, ((0, 0), (0, width - a.shape[1])))


def _mixer(x2, B, norm_mix, w_in, b_f, mix_gain, w_out, tables, tri, lm):
    w_qk = w_in[:, :QK_W].astype(BF16)
    w_vt = w_in[:, QK_W:QKV_W].T.astype(BF16)
    w_f = _pad_lanes(w_in[:, QKV_W:]).astype(BF16)
    qk, vt3, f_logit = _inproj(x2, norm_mix[None, :], w_qk, w_vt, w_f, *tables)
    cbc, crow = _forget_cumsum(f_logit.reshape(B, SEQ, LANE), _pad_lanes(b_f[None, :]), tri)
    qk3 = qk.reshape(B, SEQ, QK_W)
    oa = _attention("fox", 0, H_FOX, qk3, vt3, (cbc, crow))
    ob = _attention("moba", H_FOX, H_MOBA, qk3, vt3, ())
    oc = _attention("dil", H_FOX + H_MOBA, H_DIL, qk3, vt3, (lm,))
    a0, b0, c0 = 0, (H_FOX // 2) * PAIR, ((H_FOX + H_MOBA) // 2) * PAIR
    T = x2.shape[0]
    sl = lambda v, s: v[s:s + MIX_W]
    return _outproj(
        x2, oa.reshape(T, MIX_W), ob.reshape(T, MIX_W), oc.reshape(T, MIX_W),
        sl(mix_gain, a0)[None, :], sl(mix_gain, b0)[None, :], sl(mix_gain, c0)[None, :],
        sl(w_out, a0).astype(BF16), sl(w_out, b0).astype(BF16), sl(w_out, c0).astype(BF16))


def kernel(x, l0_norm_mix, l0_w_in, l0_b_f, l0_mix_gain, l0_w_out, l0_norm_ffn, l0_w1, l0_w3, l0_w2, l1_norm_mix, l1_w_in, l1_b_f, l1_mix_gain, l1_w_out, l1_norm_ffn, l1_w_router, l1_we1, l1_we3, l1_we2, final_norm):
    B, S, D = x.shape
    assert (S, D) == (SEQ, D_MODEL)
    T = B * S
    tables = _rope_lane_tables()
    tri = jnp.asarray(np.tril(np.ones((LANE, LANE), np.float32)), BF16)
    lm = jnp.asarray(_dilated_log2_multiplicity_t())

    x2 = x.reshape(T, D)
    x2 = _mixer(x2, B, l0_norm_mix, l0_w_in, l0_b_f, l0_mix_gain, l0_w_out, tables, tri, lm)
    x2 = _ffn(x2, l0_norm_ffn[None, :], l0_w1.astype(BF16), l0_w3.astype(BF16), l0_w2.astype(BF16))
    x2 = _mixer(x2, B, l1_norm_mix, l1_w_in, l1_b_f, l1_mix_gain, l1_w_out, tables, tri, lm)

    wr = _pad_lanes(l1_w_router)
    wr_hi = wr.astype(BF16)
    wr_lo = (wr - wr_hi.astype(F32)).astype(BF16)
    wr_cat = jnp.concatenate([wr_hi, wr_lo], axis=1)
    h, posc, gatec, posr, cnt = _router(x2, l1_norm_ffn[None, :], wr_cat, wr_hi, tri)
    ends = cnt[:, :, :N_EXPERTS].astype(jnp.int32).transpose(0, 2, 1)
    pref = jnp.pad(ends, ((0, 0), (0, 0), (1, 0))).reshape(-1)
    out = _moe(pref, h, posc, gatec, posr,
               l1_we1.astype(BF16), l1_we3.astype(BF16), l1_we2.astype(BF16), x2, final_norm[None, :])
    return out.reshape(B, S, D)
```

```python
import functools
import math

import numpy as np
import jax
import jax.numpy as jnp
from jax import lax
from jax.experimental import pallas as pl
from jax.experimental.pallas import tpu as pltpu

D_MODEL = 1024
SEQ = 2048
HEAD_DIM = 64
N_HEADS = 16
H_FOX = 5
H_MOBA = 5
H_DIL = 6
ROT_DIM = 16
ROPE_THETA = 500000.0
MOBA_BLOCK = 256
MOBA_TOPK = 3
DIL_PATTERNS = ((128, 1), (512, 4), (2048, 16))
D_FF = 3584
N_EXPERTS = 8
EPS = 1e-6

LANE = 128
PAIR = 2 * HEAD_DIM
MIX = N_HEADS * HEAD_DIM
FOX_SLOTS = 2 * ((H_FOX + 1) // 2)
QK_W = 2 * MIX
QKV_W = 3 * MIX
ATT_BLK = 256
N_ATT_BLK = SEQ // ATT_BLK
NEG = -1e30
ONES_ROWS = 16
LOG2E = math.log2(math.e)
Q_SCALE = HEAD_DIM ** -0.5 * LOG2E
VMEM_LIMIT = 56 * 1024 * 1024

BF16 = jnp.bfloat16
F32 = jnp.float32


def _dot(a, b):
    return jnp.dot(a, b, preferred_element_type=F32)


def _dot_nt(a, b):
    return lax.dot_general(a, b, (((1,), (1,)), ((), ())), preferred_element_type=F32)


def _split3(x):
    hi = x.astype(BF16)
    r1 = x - hi.astype(F32)
    mid = r1.astype(BF16)
    lo = (r1 - mid.astype(F32)).astype(BF16)
    return hi, mid, lo


def _rms(x, g):
    return x * lax.rsqrt(jnp.mean(x * x, axis=-1, keepdims=True) + EPS) * g


def _cparams(sem):
    return pltpu.CompilerParams(dimension_semantics=sem, vmem_limit_bytes=VMEM_LIMIT)


IN_TM = 1024
IN_TN = 512


def _inproj_kernel(x_ref, g_ref, w_ref, wvt_ref, wf_ref, cos_ref, sp_ref, sm_ref, qk_ref, vt_ref, f_ref):
    hb = _rms(x_ref[...], g_ref[...]).astype(BF16)
    cos, sp, sm = cos_ref[...], sp_ref[...], sm_ref[...]
    lane = lax.broadcasted_iota(jnp.int32, (1, LANE), 1)
    upper = lane >= HEAD_DIM
    n_pairs = N_HEADS // 2
    first_rot_head = H_FOX
    for c in range(QK_W // IN_TN):
        y = _dot(hb, w_ref[:, c * IN_TN:(c + 1) * IN_TN])
        for t in range(IN_TN // LANE):
            col = c * (IN_TN // LANE) + t
            part, pair = divmod(col, n_pairs)
            yt = y[:, t * LANE:(t + 1) * LANE]
            lo_head, hi_head = 2 * pair, 2 * pair + 1
            if hi_head >= first_rot_head:
                c_t, sp_t, sm_t = cos, sp, sm
                if lo_head < first_rot_head:
                    c_t = jnp.where(upper, cos, 1.0)
                    sp_t = jnp.where(upper, sp, 0.0)
                    sm_t = jnp.where(upper, sm, 0.0)
                half = ROT_DIM // 2
                yt = (yt * c_t + pltpu.roll(yt, half, axis=1) * sp_t
                      + pltpu.roll(yt, LANE - half, axis=1) * sm_t)
            if part == 0:
                yt = yt * Q_SCALE
            qk_ref[:, col * LANE:(col + 1) * LANE] = yt.astype(BF16)
    vt_ref[0] = _dot_nt(wvt_ref[...], hb).astype(BF16)
    f_ref[...] = _dot(hb, wf_ref[...])


def _inproj(x2, g, w_qk, w_vt, w_f, cos_t, sp_t, sm_t):
    T = x2.shape[0]
    n_pos = SEQ // IN_TM
    tab = pl.BlockSpec((IN_TM, LANE), lambda i: (i % n_pos, 0))
    return pl.pallas_call(
        _inproj_kernel,
        grid=(T // IN_TM,),
        in_specs=[
            pl.BlockSpec((IN_TM, D_MODEL), lambda i: (i, 0)),
            pl.BlockSpec((1, D_MODEL), lambda i: (0, 0)),
            pl.BlockSpec((D_MODEL, QK_W), lambda i: (0, 0)),
            pl.BlockSpec((MIX, D_MODEL), lambda i: (0, 0)),
            pl.BlockSpec((D_MODEL, LANE), lambda i: (0, 0)),
            tab, tab, tab,
        ],
        out_specs=[
            pl.BlockSpec((IN_TM, QK_W), lambda i: (i, 0)),
            pl.BlockSpec((1, MIX, IN_TM), lambda i: (i // n_pos, 0, i % n_pos)),
            pl.BlockSpec((IN_TM, LANE), lambda i: (i, 0)),
        ],
        out_shape=[
            jax.ShapeDtypeStruct((T, QK_W), BF16),
            jax.ShapeDtypeStruct((T // SEQ, MIX, SEQ), BF16),
            jax.ShapeDtypeStruct((T, LANE), F32),
        ],
        compiler_params=_cparams(("parallel",)),
        name="inproj",
    )(x2, g, w_qk, w_vt, w_f, cos_t, sp_t, sm_t)


def _cumsum_rows(x, tri, exact_small_ints):
    n = x.shape[0]
    carry = jnp.zeros((1, LANE), F32)
    out = []
    for b in range(n // LANE):
        xb = x[b * LANE:(b + 1) * LANE]
        if exact_small_ints:
            y = _dot(tri, xb.astype(BF16))
        else:
            hi, mid, lo = _split3(xb)
            y = _dot(tri, hi) + _dot(tri, mid) + _dot(tri, lo)
        y = y + carry
        carry = y[LANE - 1:LANE, :]
        out.append(y)
    return out


def _forget_kernel(f_ref, bf_ref, tri_ref, cbc_ref, crow_ref):
    z = f_ref[0] + bf_ref[...]
    logf = (jnp.minimum(z, 0.0) - jnp.log(1.0 + jnp.exp(-jnp.abs(z)))) * LOG2E
    blocks = _cumsum_rows(logf, tri_ref[...], exact_small_ints=False)
    for b, y in enumerate(blocks):
        yt = y.T
        for h in range(FOX_SLOTS):
            live = h < H_FOX
            cbc_ref[0, h, b * LANE:(b + 1) * LANE, :] = (
                jnp.broadcast_to(y[:, h:h + 1], (LANE, LANE)) if live else jnp.zeros((LANE, LANE), F32))
            crow_ref[0, h, :, b * LANE:(b + 1) * LANE] = yt[h:h + 1, :] if live else jnp.zeros((1, LANE), F32)


def _forget_cumsum(f3, bf_pad, tri):
    B = f3.shape[0]
    return pl.pallas_call(
        _forget_kernel,
        grid=(B,),
        in_specs=[
            pl.BlockSpec((1, SEQ, LANE), lambda b: (b, 0, 0)),
            pl.BlockSpec((1, LANE), lambda b: (0, 0)),
            pl.BlockSpec((LANE, LANE), lambda b: (0, 0)),
        ],
        out_specs=[
            pl.BlockSpec((1, FOX_SLOTS, SEQ, LANE), lambda b: (b, 0, 0, 0)),
            pl.BlockSpec((1, FOX_SLOTS, 1, SEQ), lambda b: (b, 0, 0, 0)),
        ],
        out_shape=[
            jax.ShapeDtypeStruct((B, FOX_SLOTS, SEQ, LANE), F32),
            jax.ShapeDtypeStruct((B, FOX_SLOTS, 1, SEQ), F32),
        ],
        compiler_params=_cparams(("parallel",)),
        name="forget_cumsum",
    )(f3, bf_pad, tri)


def _fold8(x, op):
    parts = [x[r:r + 8] for r in range(0, x.shape[0], 8)]
    while len(parts) > 1:
        parts = [op(parts[a], parts[a + 1]) for a in range(0, len(parts) - 1, 2)] + parts[len(parts) & ~1:]
    return parts[0]


def _attn_kernel(mode, h_first, h_last, q_ref, k_ref, vt_ref, *rest):
    if mode == "fox":
        cbc_ref, crow_ref, o_ref, s_scr = rest
    elif mode == "moba":
        o_ref, s_scr, km_scr = rest
    else:
        lmt_ref, o_ref, s_scr = rest
    pair = h_first // 2 + pl.program_id(1)
    lane = lax.broadcasted_iota(jnp.int32, (1, LANE), 1)
    key_idx = lax.broadcasted_iota(jnp.int32, (ATT_BLK, ATT_BLK), 0)
    qry_idx = lax.broadcasted_iota(jnp.int32, (ATT_BLK, ATT_BLK), 1)
    causal = key_idx <= qry_idx
    sub8 = lax.broadcasted_iota(jnp.int32, (8, ATT_BLK), 0)

    def block_means(slot, hmask):
        km_scr[slot] = jnp.zeros((LANE, LANE), F32)
        for n in range(N_ATT_BLK):
            kb = k_ref[0, n * ATT_BLK:(n + 1) * ATT_BLK, :].astype(F32)
            km = jnp.sum(kb, axis=0, keepdims=True) * (1.0 / MOBA_BLOCK)
            km_scr[slot, n:n + 1, :] = jnp.where(hmask, km, 0.0)
        return _split3(km_scr[slot])

    def query_block(slot, i, hmask, km3):
        r0 = i * ATT_BLK
        qm = jnp.where(hmask, q_ref[0, r0:r0 + ATT_BLK, :], 0.0).astype(BF16)
        if mode == "moba" and i > MOBA_TOPK:
            gt = (_dot_nt(km3[0], qm) + _dot_nt(km3[1], qm) + _dot_nt(km3[2], qm))[:8, :]
            rank = jnp.zeros((8, ATT_BLK), F32)
            for n2 in range(i):
                g2 = gt[n2:n2 + 1, :]
                ahead = (g2 > gt) | ((g2 == gt) & (n2 < sub8))
                rank = rank + jnp.where(ahead, 1.0, 0.0)
            sel_t = jnp.where((rank < MOBA_TOPK) & (sub8 < i), 1.0, 0.0)

        m8 = jnp.full((8, ATT_BLK), NEG, F32)
        for n in range(i + 1):
            c0 = n * ATT_BLK
            st = _dot_nt(k_ref[0, c0:c0 + ATT_BLK, :], qm)
            if mode == "fox":
                cs = cbc_ref[0, slot, c0:c0 + ATT_BLK, :]
                st = st - jnp.concatenate([cs, cs], axis=1)
            elif mode == "dil":
                st = st + lmt_ref[i - n]
            elif n < i and i > MOBA_TOPK:
                st = jnp.where(sel_t[n:n + 1, :] > 0.5, st, NEG)
            if n == i and mode != "dil":
                st = jnp.where(causal, st, NEG)
            s_scr[slot, n] = st
            m8 = jnp.maximum(m8, _fold8(st, jnp.maximum))
        m = jnp.max(m8, axis=0, keepdims=True)
        if mode == "fox":
            ct = crow_ref[0, slot, :, r0:r0 + ATT_BLK]
            shift = ct - (m + ct)
        else:
            shift = -m

        v_rows = slice(slot * HEAD_DIM, (slot + 1) * HEAD_DIM)
        if mode == "dil":
            acc_t = jnp.zeros((HEAD_DIM + ONES_ROWS, ATT_BLK), F32)
            ones = jnp.ones((ONES_ROWS, ATT_BLK), BF16)
            for n in range(i + 1):
                c0 = n * ATT_BLK
                p = jnp.exp2((s_scr[slot, n] + shift).astype(BF16))
                acc_t = acc_t + _dot(jnp.concatenate([vt_ref[0, v_rows, c0:c0 + ATT_BLK], ones], axis=0), p)
            return acc_t[:HEAD_DIM] * (1.0 / acc_t[HEAD_DIM:HEAD_DIM + 1])
        l8 = jnp.zeros((8, ATT_BLK), F32)
        acc_t = jnp.zeros((HEAD_DIM, ATT_BLK), F32)
        for n in range(i + 1):
            c0 = n * ATT_BLK
            p = jnp.exp2(s_scr[slot, n] + shift)
            l8 = l8 + _fold8(p, jnp.add)
            acc_t = acc_t + _dot(vt_ref[0, v_rows, c0:c0 + ATT_BLK], p.astype(BF16))
        return acc_t * (1.0 / jnp.sum(l8, axis=0, keepdims=True))

    def run(slots):
        hmasks = [(lane >= s * HEAD_DIM) & (lane < (s + 1) * HEAD_DIM) for s in (0, 1)]
        km3 = [block_means(s, hmasks[s]) if (mode == "moba" and s in slots) else None for s in (0, 1)]
        for i in range(N_ATT_BLK):
            parts = [query_block(s, i, hmasks[s], km3[s]) if s in slots
                     else jnp.zeros((HEAD_DIM, ATT_BLK), F32) for s in (0, 1)]
            o_ref[0, i * ATT_BLK:(i + 1) * ATT_BLK, :] = jnp.concatenate(parts, axis=0).T

    if h_first % 2 == 0 and h_last % 2 == 1:
        run((0, 1))
    else:
        both = (2 * pair >= h_first) & (2 * pair + 1 <= h_last)
        lone = (1,) if h_first % 2 == 1 else (0,)

        @pl.when(both)
        def _():
            run((0, 1))

        @pl.when(jnp.logical_not(both))
        def _():
            run(lone)


def _attention(mode, h_first, n_heads, qk3, vt3, extras):
    B = qk3.shape[0]
    n_pairs = N_HEADS // 2
    h_last = h_first + n_heads - 1
    pair0 = h_first // 2
    n_out_pairs = h_last // 2 - pair0 + 1

    def col(part):
        return lambda b, j: (b, 0, part * n_pairs + pair0 + j)

    in_specs = [pl.BlockSpec((1, SEQ, LANE), col(0)), pl.BlockSpec((1, SEQ, LANE), col(1)),
                pl.BlockSpec((1, LANE, SEQ), lambda b, j: (b, pair0 + j, 0))]
    scratch = [pltpu.VMEM((2, N_ATT_BLK, ATT_BLK, ATT_BLK), F32)]
    if mode == "fox":
        in_specs += [pl.BlockSpec((1, 2, SEQ, LANE), lambda b, j: (b, j, 0, 0)),
                     pl.BlockSpec((1, 2, 1, SEQ), lambda b, j: (b, j, 0, 0))]
    elif mode == "moba":
        scratch += [pltpu.VMEM((2, LANE, LANE), F32)]
    else:
        in_specs += [pl.BlockSpec((N_ATT_BLK, ATT_BLK, ATT_BLK), lambda b, j: (0, 0, 0))]
    return pl.pallas_call(
        functools.partial(_attn_kernel, mode, h_first, h_last),
        grid=(B, n_out_pairs),
        in_specs=in_specs,
        out_specs=pl.BlockSpec((1, SEQ, LANE), lambda b, j: (b, 0, j)),
        out_shape=jax.ShapeDtypeStruct((B, SEQ, n_out_pairs * LANE), F32),
        scratch_shapes=scratch,
        compiler_params=_cparams(("parallel", "parallel")),
        name="attn_" + mode,
    )(qk3, qk3, vt3, *extras)


def _dilated_log2_multiplicity_t():
    r = np.arange(ATT_BLK)[None, :]
    c = np.arange(ATT_BLK)[:, None]
    out = np.empty((N_ATT_BLK, ATT_BLK, ATT_BLK), np.float32)
    for d in range(N_ATT_BLK):
        dist = d * ATT_BLK + r - c
        mult = np.zeros_like(dist)
        for window, dil in DIL_PATTERNS:
            mult += ((dist >= 0) & (dist % dil == 0) & (dist // dil <= window // dil)).astype(dist.dtype)
        out[d] = np.where(mult > 0, np.log2(np.maximum(mult, 1)), NEG)
    return out


OUT_TM = 1024
MIX_W = 3 * LANE


def _outproj_kernel(x_ref, oa_ref, ob_ref, oc_ref, ga_ref, gb_ref, gc_ref, wa_ref, wb_ref, wc_ref, o_ref):
    def normed(o_r, g_r, width):
        o = o_r[...]
        ms = jnp.sum(o * o, axis=-1, keepdims=True) * (1.0 / width)
        return (o * lax.rsqrt(ms + EPS) * g_r[...]).astype(BF16)

    ya = normed(oa_ref, ga_ref, H_FOX * HEAD_DIM)
    yb = normed(ob_ref, gb_ref, H_MOBA * HEAD_DIM)
    yc = normed(oc_ref, gc_ref, H_DIL * HEAD_DIM)
    o_ref[...] = x_ref[...] + _dot(ya, wa_ref[...]) + _dot(yb, wb_ref[...]) + _dot(yc, wc_ref[...])


def _outproj(x2, oa, ob, oc, ga, gb, gc, wa, wb, wc):
    T = x2.shape[0]
    row = lambda w: pl.BlockSpec((OUT_TM, w), lambda i: (i, 0))
    full = lambda a, b: pl.BlockSpec((a, b), lambda i: (0, 0))
    return pl.pallas_call(
        _outproj_kernel,
        grid=(T // OUT_TM,),
        in_specs=[row(D_MODEL), row(MIX_W), row(MIX_W), row(MIX_W),
                  full(1, MIX_W), full(1, MIX_W), full(1, MIX_W),
                  full(MIX_W, D_MODEL), full(MIX_W, D_MODEL), full(MIX_W, D_MODEL)],
        out_specs=row(D_MODEL),
        out_shape=jax.ShapeDtypeStruct((T, D_MODEL), F32),
        compiler_params=_cparams(("parallel",)),
        name="outproj",
    )(x2, oa, ob, oc, ga, gb, gc, wa, wb, wc)


FFN_TM = 2048
FFN_TF = 512
FFN_SUB = 256


def _swiglu_act(a, b):
    return a * (1.0 / (1.0 + jnp.exp(-a))) * b


def _ffn_kernel(x_ref, g_ref, w1_ref, w3_ref, w2_ref, o_ref, h_scr):
    f = pl.program_id(1)

    @pl.when(f == 0)
    def _():
        x = x_ref[...]
        h_scr[...] = _rms(x, g_ref[...]).astype(BF16)
        o_ref[...] = x

    for c in range(FFN_TM // FFN_SUB):
        rows = slice(c * FFN_SUB, (c + 1) * FFN_SUB)
        h = h_scr[rows, :]
        act = _swiglu_act(_dot(h, w1_ref[...]), _dot(h, w3_ref[...])).astype(BF16)
        o_ref[rows, :] = o_ref[rows, :] + _dot(act, w2_ref[...])


def _ffn(x2, g, w1, w3, w2):
    T = x2.shape[0]
    return pl.pallas_call(
        _ffn_kernel,
        grid=(T // FFN_TM, D_FF // FFN_TF),
        in_specs=[
            pl.BlockSpec((FFN_TM, D_MODEL), lambda i, f: (i, 0)),
            pl.BlockSpec((1, D_MODEL), lambda i, f: (0, 0)),
            pl.BlockSpec((D_MODEL, FFN_TF), lambda i, f: (0, f)),
            pl.BlockSpec((D_MODEL, FFN_TF), lambda i, f: (0, f)),
            pl.BlockSpec((FFN_TF, D_MODEL), lambda i, f: (f, 0)),
        ],
        out_specs=pl.BlockSpec((FFN_TM, D_MODEL), lambda i, f: (i, 0)),
        out_shape=jax.ShapeDtypeStruct((T, D_MODEL), F32),
        scratch_shapes=[pltpu.VMEM((FFN_TM, D_MODEL), BF16)],
        compiler_params=_cparams(("parallel", "arbitrary")),
        name="ffn_dense",
    )(x2, g, w1, w3, w2)


MOE_TT = 2048
MOE_CH = 256
MOE_UNIT = 64
MOE_TF = 512
MOE_KB = 256
MOE_NKB = MOE_TT // MOE_KB
MOE_X_SLOTS = 4
MOE_GATHER_WIDTHS = (2, 6)


def _router_kernel(x_ref, g_ref, wr_cat_ref, wr_hi_ref, tri_ref,
                   h_ref, posc_ref, gatec_ref, posr_ref, cnt_ref):
    h = _rms(x_ref[...], g_ref[...])
    h_hi = h.astype(BF16)
    h_lo = (h - h_hi.astype(F32)).astype(BF16)
    h_ref[...] = h_hi
    hh = _dot(h_hi, wr_cat_ref[...])
    logits = hh[:, :LANE] + hh[:, LANE:] + _dot(h_lo, wr_hi_ref[...])
    lane = lax.broadcasted_iota(jnp.int32, (1, LANE), 1).astype(F32)
    lg = jnp.where(lane < N_EXPERTS, logits, -jnp.inf)
    m1 = jnp.max(lg, axis=1, keepdims=True)
    i1 = jnp.min(jnp.where(lg == m1, lane, float(LANE)), axis=1, keepdims=True)
    lg2 = jnp.where(lane == i1, -jnp.inf, lg)
    m2 = jnp.max(lg2, axis=1, keepdims=True)
    i2 = jnp.min(jnp.where(lg2 == m2, lane, float(LANE)), axis=1, keepdims=True)
    e2 = jnp.exp(m2 - m1)
    g1 = 1.0 / (1.0 + e2)
    g2 = e2 * g1
    routed = (lane == i1) | (lane == i2)
    gatec_ref[...] = jnp.where(lane == i1, g1, jnp.where(lane == i2, g2, 0.0))
    routed_f = jnp.where(routed, 1.0, 0.0)
    blocks = _cumsum_rows(routed_f, tri_ref[...], exact_small_ints=True)
    for b, cnt in enumerate(blocks):
        sl = slice(b * LANE, (b + 1) * LANE)
        pos = jnp.where(routed_f[sl] > 0.5, cnt - 1.0, -1.0)
        posc_ref[sl, :] = pos
        posr_ref[0, :, sl] = pos.T[:8, :]
    per_kb = MOE_KB // LANE
    for kb in range(MOE_NKB):
        cnt_ref[0, kb:kb + 1, :] = blocks[(kb + 1) * per_kb - 1][LANE - 1:LANE, :]


def _router(x2, g, wr_cat, wr_hi, tri):
    T = x2.shape[0]
    nt = T // MOE_TT
    return pl.pallas_call(
        _router_kernel,
        grid=(nt,),
        in_specs=[
            pl.BlockSpec((MOE_TT, D_MODEL), lambda i: (i, 0)),
            pl.BlockSpec((1, D_MODEL), lambda i: (0, 0)),
            pl.BlockSpec((D_MODEL, 2 * LANE), lambda i: (0, 0)),
            pl.BlockSpec((D_MODEL, LANE), lambda i: (0, 0)),
            pl.BlockSpec((LANE, LANE), lambda i: (0, 0)),
        ],
        out_specs=[
            pl.BlockSpec((MOE_TT, D_MODEL), lambda i: (i, 0)),
            pl.BlockSpec((MOE_TT, LANE), lambda i: (i, 0)),
            pl.BlockSpec((MOE_TT, LANE), lambda i: (i, 0)),
            pl.BlockSpec((1, 8, MOE_TT), lambda i: (i, 0, 0)),
            pl.BlockSpec((1, MOE_NKB, LANE), lambda i: (i, 0, 0)),
        ],
        out_shape=[
            jax.ShapeDtypeStruct((T, D_MODEL), BF16),
            jax.ShapeDtypeStruct((T, LANE), F32),
            jax.ShapeDtypeStruct((T, LANE), F32),
            jax.ShapeDtypeStruct((nt, 8, MOE_TT), F32),
            jax.ShapeDtypeStruct((nt, MOE_NKB, LANE), F32),
        ],
        compiler_params=_cparams(("parallel",)),
        name="moe_router",
    )(x2, g, wr_cat, wr_hi, tri)


def _moe_kernel(pref_ref, h_ref, posc_ref, gatec_ref, posr_ref, w1_ref, w3_ref, w2_ref, x_hbm, gfin_ref, o_ref,
                xe_scr, ye_scr, pe_scr, ge_scr, x_buf, x_sem):
    t, e, f = pl.program_id(0), pl.program_id(1), pl.program_id(2)
    nf = pl.num_programs(2)
    base = (t * N_EXPERTS + e) * (MOE_NKB + 1)
    pref = [pref_ref[base + kb] for kb in range(MOE_NKB + 1)]
    n_rows = pref[MOE_NKB]
    lane = lax.broadcasted_iota(jnp.int32, (1, LANE), 1)
    last_step = (f == nf - 1) & (e == N_EXPERTS - 1)

    def x_copy(kb):
        src = x_hbm.at[pl.ds(pl.multiple_of(t * MOE_TT + kb * MOE_KB, MOE_KB), MOE_KB), :]
        return pltpu.make_async_copy(src, x_buf.at[kb % MOE_X_SLOTS], x_sem.at[kb % MOE_X_SLOTS])

    @pl.when(last_step)
    def _():
        for kb in range(MOE_X_SLOTS):
            x_copy(kb).start()

    @pl.when((e == 0) & (f == 0))
    def _():
        o_ref[...] = jnp.zeros((MOE_TT, D_MODEL), F32)

    @pl.when(f == 0)
    def _():
        is_e = lane == e
        pe_scr[...] = jnp.sum(jnp.where(is_e, posc_ref[...], 0.0), axis=1, keepdims=True)
        ge_scr[...] = jnp.sum(jnp.where(is_e, gatec_ref[...], 0.0), axis=1, keepdims=True)
        slot = lax.broadcasted_iota(jnp.int32, (MOE_CH, 1), 0).astype(F32)

        def gather(c, _):
            r0 = pl.multiple_of(c * MOE_CH, MOE_CH)
            rows = pl.ds(r0, MOE_CH)
            slots = slot + r0.astype(F32)
            ye_scr[rows, :] = jnp.zeros((MOE_CH, D_MODEL), F32)
            kb_lo = sum((pref[kb + 1] <= r0).astype(jnp.int32) for kb in range(MOE_NKB))
            kb_hi = MOE_NKB - 1 - sum((pref[kb] >= r0 + MOE_CH).astype(jnp.int32) for kb in range(MOE_NKB))
            n_kb = kb_hi - kb_lo + 1

            def one_dot(width):
                t0 = pl.multiple_of(jnp.minimum(kb_lo, MOE_NKB - width) * MOE_KB, MOE_KB)
                tok = pl.ds(t0, width * MOE_KB)
                onehot = jnp.where(posr_ref[0, pl.ds(e, 1), tok] == slots, 1.0, 0.0).astype(BF16)
                xe_scr[rows, :] = _dot(onehot, h_ref[tok, :]).astype(BF16)

            lo_w, mid_w = MOE_GATHER_WIDTHS
            pl.when(n_kb <= lo_w)(lambda: one_dot(lo_w))
            pl.when((n_kb > lo_w) & (n_kb <= mid_w))(lambda: one_dot(mid_w))
            pl.when(n_kb > mid_w)(lambda: one_dot(MOE_NKB))
            return 0

        lax.fori_loop(0, (n_rows + MOE_CH - 1) // MOE_CH, gather, 0)

    def expert_rows(r0, n):
        rows = pl.ds(r0, n)
        xc = xe_scr[rows, :]
        act = _swiglu_act(_dot(xc, w1_ref[0]), _dot(xc, w3_ref[0])).astype(BF16)
        ye_scr[rows, :] = ye_scr[rows, :] + _dot(act, w2_ref[0])

    n_units = (n_rows + MOE_UNIT - 1) // MOE_UNIT
    units_per_trip = 2 * MOE_CH // MOE_UNIT
    half = MOE_CH // 2

    def chunks(r0, sizes):
        for n in sizes:
            expert_rows(r0, n)
            r0 = r0 + n

    def two_chunks(i, _):
        chunks(pl.multiple_of(i * (2 * MOE_CH), 2 * MOE_CH), (MOE_CH, MOE_CH))
        return 0

    n_trips = n_units // units_per_trip
    left = n_units - n_trips * units_per_trip
    ride = (left >= 1) & (left <= 2) & (n_trips >= 1)
    n_plain = n_trips - ride.astype(jnp.int32)
    lax.fori_loop(0, n_plain, two_chunks, 0)
    tail0 = pl.multiple_of(n_plain * (2 * MOE_CH), 2 * MOE_CH)
    pl.when(ride & (left == 1))(lambda: chunks(tail0, (MOE_CH, MOE_CH + MOE_UNIT)))
    pl.when(ride & (left == 2))(lambda: chunks(tail0, (MOE_CH, MOE_CH + 2 * MOE_UNIT)))
    tail_halves = jnp.where(ride, 0, (left * MOE_UNIT + half - 1) // half)
    pl.when(tail_halves == 1)(lambda: chunks(tail0, (half,)))
    pl.when(tail_halves == 2)(lambda: chunks(tail0, (MOE_CH,)))
    pl.when(tail_halves == 3)(lambda: chunks(tail0, (MOE_CH, half)))
    pl.when(tail_halves == 4)(lambda: chunks(tail0, (MOE_CH, MOE_CH)))

    @pl.when(f == nf - 1)
    def _():
        for kb in range(MOE_NKB):
            lo, hi = pref[kb], pref[kb + 1]
            tok = slice(kb * MOE_KB, (kb + 1) * MOE_KB)
            w_lo = lo // MOE_CH
            w_hi = (hi - 1) // MOE_CH

            def combine(n_win, w0, tok=tok):
                r0 = pl.multiple_of(w0 * MOE_CH, MOE_CH)
                slots = (lax.broadcasted_iota(jnp.int32, (1, n_win * MOE_CH), 1) + r0).astype(F32)
                onehot_t = jnp.where(pe_scr[tok, :] == slots, 1.0, 0.0).astype(BF16)
                y = ye_scr[pl.ds(r0, n_win * MOE_CH), :].astype(BF16)
                o_ref[tok, :] = o_ref[tok, :] + ge_scr[tok, :] * _dot(onehot_t, y)

            pl.when((hi > lo) & (w_hi == w_lo))(functools.partial(combine, 1, w_lo))
            pl.when((hi > lo) & (w_hi > w_lo))(functools.partial(combine, 2, w_lo))

    @pl.when(last_step)
    def _():
        for kb in range(MOE_NKB):
            x_copy(kb).wait()
            tok = slice(kb * MOE_KB, (kb + 1) * MOE_KB)
            o_ref[tok, :] = _rms(x_buf[kb % MOE_X_SLOTS] + o_ref[tok, :], gfin_ref[...])
            if kb + MOE_X_SLOTS < MOE_NKB:
                x_copy(kb + MOE_X_SLOTS).start()


def _moe(counts, h, posc, gatec, posr, we1, we3, we2, x2, g_final):
    T = h.shape[0]
    nt = T // MOE_TT
    grid_spec = pltpu.PrefetchScalarGridSpec(
        num_scalar_prefetch=1,
        grid=(nt, N_EXPERTS, D_FF // MOE_TF),
        in_specs=[
            pl.BlockSpec((MOE_TT, D_MODEL), lambda t, e, f, c: (t, 0)),
            pl.BlockSpec((MOE_TT, LANE), lambda t, e, f, c: (t, 0)),
            pl.BlockSpec((MOE_TT, LANE), lambda t, e, f, c: (t, 0)),
            pl.BlockSpec((1, 8, MOE_TT), lambda t, e, f, c: (t, 0, 0)),
            pl.BlockSpec((1, D_MODEL, MOE_TF), lambda t, e, f, c: (e, 0, f)),
            pl.BlockSpec((1, D_MODEL, MOE_TF), lambda t, e, f, c: (e, 0, f)),
            pl.BlockSpec((1, MOE_TF, D_MODEL), lambda t, e, f, c: (e, f, 0)),
            pl.BlockSpec(memory_space=pl.ANY),
            pl.BlockSpec((1, D_MODEL), lambda t, e, f, c: (0, 0)),
        ],
        out_specs=pl.BlockSpec((MOE_TT, D_MODEL), lambda t, e, f, c: (t, 0)),
        scratch_shapes=[
            pltpu.VMEM((MOE_TT, D_MODEL), BF16),
            pltpu.VMEM((MOE_TT, D_MODEL), F32),
            pltpu.VMEM((MOE_TT, 1), F32),
            pltpu.VMEM((MOE_TT, 1), F32),
            pltpu.VMEM((MOE_X_SLOTS, MOE_KB, D_MODEL), F32),
            pltpu.SemaphoreType.DMA((MOE_X_SLOTS,)),
        ],
    )
    return pl.pallas_call(
        _moe_kernel,
        grid_spec=grid_spec,
        out_shape=jax.ShapeDtypeStruct((T, D_MODEL), F32),
        compiler_params=_cparams(("parallel", "arbitrary", "arbitrary")),
        name="moe_experts",
    )(counts, h, posc, gatec, posr, we1, we3, we2, x2, g_final)


def _rope_lane_tables():
    half = ROT_DIM // 2
    inv = ROPE_THETA ** (-jnp.arange(0, ROT_DIM, 2, dtype=F32) / ROT_DIM)
    ang = jnp.arange(SEQ, dtype=F32)[:, None] * inv[None, :]
    cos, sin = jnp.cos(ang), jnp.sin(ang)
    ones = jnp.ones((SEQ, HEAD_DIM - ROT_DIM), F32)
    zeros = jnp.zeros((SEQ, HEAD_DIM - ROT_DIM), F32)
    z8 = jnp.zeros((SEQ, half), F32)
    cos_h = jnp.concatenate([cos, cos, ones], axis=1)
    sp_h = jnp.concatenate([z8, sin, zeros], axis=1)
    sm_h = jnp.concatenate([-sin, z8, zeros], axis=1)
    two = lambda a: jnp.concatenate([a, a], axis=1)
    return two(cos_h), two(sp_h), two(sm_h)


def _pad_lanes(a, width=LANE):
    return jnp.pad(a, ((0, 0), (0, width - a.shape[1])))


def _mixer(x2, B, norm_mix, w_in, b_f, mix_gain, w_out, tables, tri, lm):
    w_qk = w_in[:, :QK_W].astype(BF16)
    w_vt = w_in[:, QK_W:QKV_W].T.astype(BF16)
    w_f = _pad_lanes(w_in[:, QKV_W:]).astype(BF16)
    qk, vt3, f_logit = _inproj(x2, norm_mix[None, :], w_qk, w_vt, w_f, *tables)
    cbc, crow = _forget_cumsum(f_logit.reshape(B, SEQ, LANE), _pad_lanes(b_f[None, :]), tri)
    qk3 = qk.reshape(B, SEQ, QK_W)
    oa = _attention("fox", 0, H_FOX, qk3, vt3, (cbc, crow))
    ob = _attention("moba", H_FOX, H_MOBA, qk3, vt3, ())
    oc = _attention("dil", H_FOX + H_MOBA, H_DIL, qk3, vt3, (lm,))
    a0, b0, c0 = 0, (H_FOX // 2) * PAIR, ((H_FOX + H_MOBA) // 2) * PAIR
    T = x2.shape[0]
    sl = lambda v, s: v[s:s + MIX_W]
    return _outproj(
        x2, oa.reshape(T, MIX_W), ob.reshape(T, MIX_W), oc.reshape(T, MIX_W),
        sl(mix_gain, a0)[None, :], sl(mix_gain, b0)[None, :], sl(mix_gain, c0)[None, :],
        sl(w_out, a0).astype(BF16), sl(w_out, b0).astype(BF16), sl(w_out, c0).astype(BF16))


def kernel(x, l0_norm_mix, l0_w_in, l0_b_f, l0_mix_gain, l0_w_out, l0_norm_ffn, l0_w1, l0_w3, l0_w2, l1_norm_mix, l1_w_in, l1_b_f, l1_mix_gain, l1_w_out, l1_norm_ffn, l1_w_router, l1_we1, l1_we3, l1_we2, final_norm):
    B, S, D = x.shape
    assert (S, D) == (SEQ, D_MODEL)
    T = B * S
    tables = _rope_lane_tables()
    tri = jnp.asarray(np.tril(np.ones((LANE, LANE), np.float32)), BF16)
    lm = jnp.asarray(_dilated_log2_multiplicity_t())

    x2 = x.reshape(T, D)
    x2 = _mixer(x2, B, l0_norm_mix, l0_w_in, l0_b_f, l0_mix_gain, l0_w_out, tables, tri, lm)
    x2 = _ffn(x2, l0_norm_ffn[None, :], l0_w1.astype(BF16), l0_w3.astype(BF16), l0_w2.astype(BF16))
    x2 = _mixer(x2, B, l1_norm_mix, l1_w_in, l1_b_f, l1_mix_gain, l1_w_out, tables, tri, lm)

    wr = _pad_lanes(l1_w_router)
    wr_hi = wr.astype(BF16)
    wr_lo = (wr - wr_hi.astype(F32)).astype(BF16)
    wr_cat = jnp.concatenate([wr_hi, wr_lo], axis=1)
    h, posc, gatec, posr, cnt = _router(x2, l1_norm_ffn[None, :], wr_cat, wr_hi, tri)
    ends = cnt[:, :, :N_EXPERTS].astype(jnp.int32).transpose(0, 2, 1)
    pref = jnp.pad(ends, ((0, 0), (0, 0), (1, 0))).reshape(-1)
    out = _moe(pref, h, posc, gatec, posr,
               l1_we1.astype(BF16), l1_we3.astype(BF16), l1_we2.astype(BF16), x2, final_norm[None, :])
    return out.reshape(B, S, D)
```

```python
import functools
import math

import numpy as np
import jax
import jax.numpy as jnp
from jax import lax
from jax.experimental import pallas as pl
from jax.experimental.pallas import tpu as pltpu

D_MODEL = 1024
SEQ = 2048
HEAD_DIM = 64
N_HEADS = 16
H_FOX = 5
H_MOBA = 5
H_DIL = 6
ROT_DIM = 16
ROPE_THETA = 500000.0
MOBA_BLOCK = 256
MOBA_TOPK = 3
DIL_PATTERNS = ((128, 1), (512, 4), (2048, 16))
D_FF = 3584
N_EXPERTS = 8
EPS = 1e-6

LANE = 128
PAIR = 2 * HEAD_DIM
MIX = N_HEADS * HEAD_DIM
FOX_SLOTS = 2 * ((H_FOX + 1) // 2)
QK_W = 2 * MIX
QKV_W = 3 * MIX
ATT_BLK = 256
N_ATT_BLK = SEQ // ATT_BLK
NEG = -1e30
ONES_ROWS = 16
LOG2E = math.log2(math.e)
Q_SCALE = HEAD_DIM ** -0.5 * LOG2E
VMEM_LIMIT = 56 * 1024 * 1024

BF16 = jnp.bfloat16
F32 = jnp.float32


def _dot(a, b):
    return jnp.dot(a, b, preferred_element_type=F32)


def _dot_nt(a, b):
    return lax.dot_general(a, b, (((1,), (1,)), ((), ())), preferred_element_type=F32)


def _split3(x):
    hi = x.astype(BF16)
    r1 = x - hi.astype(F32)
    mid = r1.astype(BF16)
    lo = (r1 - mid.astype(F32)).astype(BF16)
    return hi, mid, lo


def _rms(x, g):
    return x * lax.rsqrt(jnp.mean(x * x, axis=-1, keepdims=True) + EPS) * g


def _cparams(sem):
    return pltpu.CompilerParams(dimension_semantics=sem, vmem_limit_bytes=VMEM_LIMIT)


IN_TM = 1024
IN_TN = 512


def _inproj_kernel(x_ref, g_ref, w_ref, wvt_ref, wf_ref, cos_ref, sp_ref, sm_ref, qk_ref, vt_ref, f_ref):
    hb = _rms(x_ref[...], g_ref[...]).astype(BF16)
    cos, sp, sm = cos_ref[...], sp_ref[...], sm_ref[...]
    lane = lax.broadcasted_iota(jnp.int32, (1, LANE), 1)
    upper = lane >= HEAD_DIM
    n_pairs = N_HEADS // 2
    first_rot_head = H_FOX
    for c in range(QK_W // IN_TN):
        y = _dot(hb, w_ref[:, c * IN_TN:(c + 1) * IN_TN])
        for t in range(IN_TN // LANE):
            col = c * (IN_TN // LANE) + t
            part, pair = divmod(col, n_pairs)
            yt = y[:, t * LANE:(t + 1) * LANE]
            lo_head, hi_head = 2 * pair, 2 * pair + 1
            if hi_head >= first_rot_head:
                c_t, sp_t, sm_t = cos, sp, sm
                if lo_head < first_rot_head:
                    c_t = jnp.where(upper, cos, 1.0)
                    sp_t = jnp.where(upper, sp, 0.0)
                    sm_t = jnp.where(upper, sm, 0.0)
                half = ROT_DIM // 2
                yt = (yt * c_t + pltpu.roll(yt, half, axis=1) * sp_t
                      + pltpu.roll(yt, LANE - half, axis=1) * sm_t)
            if part == 0:
                yt = yt * Q_SCALE
            qk_ref[:, col * LANE:(col + 1) * LANE] = yt.astype(BF16)
    vt_ref[0] = _dot_nt(wvt_ref[...], hb).astype(BF16)
    f_ref[...] = _dot(hb, wf_ref[...])


def _inproj(x2, g, w_qk, w_vt, w_f, cos_t, sp_t, sm_t):
    T = x2.shape[0]
    n_pos = SEQ // IN_TM
    tab = pl.BlockSpec((IN_TM, LANE), lambda i: (i % n_pos, 0))
    return pl.pallas_call(
        _inproj_kernel,
        grid=(T // IN_TM,),
        in_specs=[
            pl.BlockSpec((IN_TM, D_MODEL), lambda i: (i, 0)),
            pl.BlockSpec((1, D_MODEL), lambda i: (0, 0)),
            pl.BlockSpec((D_MODEL, QK_W), lambda i: (0, 0)),
            pl.BlockSpec((MIX, D_MODEL), lambda i: (0, 0)),
            pl.BlockSpec((D_MODEL, LANE), lambda i: (0, 0)),
            tab, tab, tab,
        ],
        out_specs=[
            pl.BlockSpec((IN_TM, QK_W), lambda i: (i, 0)),
            pl.BlockSpec((1, MIX, IN_TM), lambda i: (i // n_pos, 0, i % n_pos)),
            pl.BlockSpec((IN_TM, LANE), lambda i: (i, 0)),
        ],
        out_shape=[
            jax.ShapeDtypeStruct((T, QK_W), BF16),
            jax.ShapeDtypeStruct((T // SEQ, MIX, SEQ), BF16),
            jax.ShapeDtypeStruct((T, LANE), F32),
        ],
        compiler_params=_cparams(("parallel",)),
        name="inproj",
    )(x2, g, w_qk, w_vt, w_f, cos_t, sp_t, sm_t)


def _cumsum_rows(x, tri, exact_small_ints):
    n = x.shape[0]
    carry = jnp.zeros((1, LANE), F32)
    out = []
    for b in range(n // LANE):
        xb = x[b * LANE:(b + 1) * LANE]
        if exact_small_ints:
            y = _dot(tri, xb.astype(BF16))
        else:
            hi, mid, lo = _split3(xb)
            y = _dot(tri, hi) + _dot(tri, mid) + _dot(tri, lo)
        y = y + carry
        carry = y[LANE - 1:LANE, :]
        out.append(y)
    return out


def _forget_kernel(f_ref, bf_ref, tri_ref, cbc_ref, crow_ref):
    z = f_ref[0] + bf_ref[...]
    logf = (jnp.minimum(z, 0.0) - jnp.log(1.0 + jnp.exp(-jnp.abs(z)))) * LOG2E
    blocks = _cumsum_rows(logf, tri_ref[...], exact_small_ints=False)
    for b, y in enumerate(blocks):
        yt = y.T
        for h in range(FOX_SLOTS):
            live = h < H_FOX
            cbc_ref[0, h, b * LANE:(b + 1) * LANE, :] = (
                jnp.broadcast_to(y[:, h:h + 1], (LANE, LANE)) if live else jnp.zeros((LANE, LANE), F32))
            crow_ref[0, h, :, b * LANE:(b + 1) * LANE] = yt[h:h + 1, :] if live else jnp.zeros((1, LANE), F32)


def _forget_cumsum(f3, bf_pad, tri):
    B = f3.shape[0]
    return pl.pallas_call(
        _forget_kernel,
        grid=(B,),
        in_specs=[
            pl.BlockSpec((1, SEQ, LANE), lambda b: (b, 0, 0)),
            pl.BlockSpec((1, LANE), lambda b: (0, 0)),
            pl.BlockSpec((LANE, LANE), lambda b: (0, 0)),
        ],
        out_specs=[
            pl.BlockSpec((1, FOX_SLOTS, SEQ, LANE), lambda b: (b, 0, 0, 0)),
            pl.BlockSpec((1, FOX_SLOTS, 1, SEQ), lambda b: (b, 0, 0, 0)),
        ],
        out_shape=[
            jax.ShapeDtypeStruct((B, FOX_SLOTS, SEQ, LANE), F32),
            jax.ShapeDtypeStruct((B, FOX_SLOTS, 1, SEQ), F32),
        ],
        compiler_params=_cparams(("parallel",)),
        name="forget_cumsum",
    )(f3, bf_pad, tri)


def _fold8(x, op):
    parts = [x[r:r + 8] for r in range(0, x.shape[0], 8)]
    while len(parts) > 1:
        parts = [op(parts[a], parts[a + 1]) for a in range(0, len(parts) - 1, 2)] + parts[len(parts) & ~1:]
    return parts[0]


def _attn_kernel(mode, h_first, h_last, q_ref, k_ref, vt_ref, *rest):
    if mode == "fox":
        cbc_ref, crow_ref, o_ref, s_scr = rest
    elif mode == "moba":
        o_ref, s_scr, km_scr = rest
    else:
        lmt_ref, o_ref, s_scr = rest
    pair = h_first // 2 + pl.program_id(1)
    lane = lax.broadcasted_iota(jnp.int32, (1, LANE), 1)
    key_idx = lax.broadcasted_iota(jnp.int32, (ATT_BLK, ATT_BLK), 0)
    qry_idx = lax.broadcasted_iota(jnp.int32, (ATT_BLK, ATT_BLK), 1)
    causal = key_idx <= qry_idx
    sub8 = lax.broadcasted_iota(jnp.int32, (8, ATT_BLK), 0)

    def block_means(slot, hmask):
        km_scr[slot] = jnp.zeros((LANE, LANE), F32)
        for n in range(N_ATT_BLK):
            kb = k_ref[0, n * ATT_BLK:(n + 1) * ATT_BLK, :].astype(F32)
            km = jnp.sum(kb, axis=0, keepdims=True) * (1.0 / MOBA_BLOCK)
            km_scr[slot, n:n + 1, :] = jnp.where(hmask, km, 0.0)
        return _split3(km_scr[slot])

    def query_block(slot, i, hmask, km3):
        r0 = i * ATT_BLK
        qm = jnp.where(hmask, q_ref[0, r0:r0 + ATT_BLK, :], 0.0).astype(BF16)
        if mode == "moba" and i > MOBA_TOPK:
            gt = (_dot_nt(km3[0], qm) + _dot_nt(km3[1], qm) + _dot_nt(km3[2], qm))[:8, :]
            rank = jnp.zeros((8, ATT_BLK), F32)
            for n2 in range(i):
                g2 = gt[n2:n2 + 1, :]
                ahead = (g2 > gt) | ((g2 == gt) & (n2 < sub8))
                rank = rank + jnp.where(ahead, 1.0, 0.0)
            sel_t = jnp.where((rank < MOBA_TOPK) & (sub8 < i), 1.0, 0.0)

        m8 = jnp.full((8, ATT_BLK), NEG, F32)
        for n in range(i + 1):
            c0 = n * ATT_BLK
            st = _dot_nt(k_ref[0, c0:c0 + ATT_BLK, :], qm)
            if mode == "fox":
                cs = cbc_ref[0, slot, c0:c0 + ATT_BLK, :]
                st = st - jnp.concatenate([cs, cs], axis=1)
            elif mode == "dil":
                st = st + lmt_ref[i - n]
            elif n < i and i > MOBA_TOPK:
                st = jnp.where(sel_t[n:n + 1, :] > 0.5, st, NEG)
            if n == i and mode != "dil":
                st = jnp.where(causal, st, NEG)
            s_scr[slot, n] = st
            m8 = jnp.maximum(m8, _fold8(st, jnp.maximum))
        m = jnp.max(m8, axis=0, keepdims=True)
        if mode == "fox":
            ct = crow_ref[0, slot, :, r0:r0 + ATT_BLK]
            shift = ct - (m + ct)
        else:
            shift = -m

        v_rows = slice(slot * HEAD_DIM, (slot + 1) * HEAD_DIM)
        if mode == "dil":
            acc_t = jnp.zeros((HEAD_DIM + ONES_ROWS, ATT_BLK), F32)
            ones = jnp.ones((ONES_ROWS, ATT_BLK), BF16)
            for n in range(i + 1):
                c0 = n * ATT_BLK
                p = jnp.exp2((s_scr[slot, n] + shift).astype(BF16))
                acc_t = acc_t + _dot(jnp.concatenate([vt_ref[0, v_rows, c0:c0 + ATT_BLK], ones], axis=0), p)
            return acc_t[:HEAD_DIM] * (1.0 / acc_t[HEAD_DIM:HEAD_DIM + 1])
        l8 = jnp.zeros((8, ATT_BLK), F32)
        acc_t = jnp.zeros((HEAD_DIM, ATT_BLK), F32)
        for n in range(i + 1):
            c0 = n * ATT_BLK
            p = jnp.exp2(s_scr[slot, n] + shift)
            l8 = l8 + _fold8(p, jnp.add)
            acc_t = acc_t + _dot(vt_ref[0, v_rows, c0:c0 + ATT_BLK], p.astype(BF16))
        return acc_t * (1.0 / jnp.sum(l8, axis=0, keepdims=True))

    def run(slots):
        hmasks = [(lane >= s * HEAD_DIM) & (lane < (s + 1) * HEAD_DIM) for s in (0, 1)]
        km3 = [block_means(s, hmasks[s]) if (mode == "moba" and s in slots) else None for s in (0, 1)]
        for i in range(N_ATT_BLK):
            parts = [query_block(s, i, hmasks[s], km3[s]) if s in slots
                     else jnp.zeros((HEAD_DIM, ATT_BLK), F32) for s in (0, 1)]
            o_ref[0, i * ATT_BLK:(i + 1) * ATT_BLK, :] = jnp.concatenate(parts, axis=0).T

    if h_first % 2 == 0 and h_last % 2 == 1:
        run((0, 1))
    else:
        both = (2 * pair >= h_first) & (2 * pair + 1 <= h_last)
        lone = (1,) if h_first % 2 == 1 else (0,)

        @pl.when(both)
        def _():
            run((0, 1))

        @pl.when(jnp.logical_not(both))
        def _():
            run(lone)


def _attention(mode, h_first, n_heads, qk3, vt3, extras):
    B = qk3.shape[0]
    n_pairs = N_HEADS // 2
    h_last = h_first + n_heads - 1
    pair0 = h_first // 2
    n_out_pairs = h_last // 2 - pair0 + 1

    def col(part):
        return lambda b, j: (b, 0, part * n_pairs + pair0 + j)

    in_specs = [pl.BlockSpec((1, SEQ, LANE), col(0)), pl.BlockSpec((1, SEQ, LANE), col(1)),
                pl.BlockSpec((1, LANE, SEQ), lambda b, j: (b, pair0 + j, 0))]
    scratch = [pltpu.VMEM((2, N_ATT_BLK, ATT_BLK, ATT_BLK), F32)]
    if mode == "fox":
        in_specs += [pl.BlockSpec((1, 2, SEQ, LANE), lambda b, j: (b, j, 0, 0)),
                     pl.BlockSpec((1, 2, 1, SEQ), lambda b, j: (b, j, 0, 0))]
    elif mode == "moba":
        scratch += [pltpu.VMEM((2, LANE, LANE), F32)]
    else:
        in_specs += [pl.BlockSpec((N_ATT_BLK, ATT_BLK, ATT_BLK), lambda b, j: (0, 0, 0))]
    return pl.pallas_call(
        functools.partial(_attn_kernel, mode, h_first, h_last),
        grid=(B, n_out_pairs),
        in_specs=in_specs,
        out_specs=pl.BlockSpec((1, SEQ, LANE), lambda b, j: (b, 0, j)),
        out_shape=jax.ShapeDtypeStruct((B, SEQ, n_out_pairs * LANE), F32),
        scratch_shapes=scratch,
        compiler_params=_cparams(("parallel", "parallel")),
        name="attn_" + mode,
    )(qk3, qk3, vt3, *extras)


def _dilated_log2_multiplicity_t():
    r = np.arange(ATT_BLK)[None, :]
    c = np.arange(ATT_BLK)[:, None]
    out = np.empty((N_ATT_BLK, ATT_BLK, ATT_BLK), np.float32)
    for d in range(N_ATT_BLK):
        dist = d * ATT_BLK + r - c
        mult = np.zeros_like(dist)
        for window, dil in DIL_PATTERNS:
            mult += ((dist >= 0) & (dist % dil == 0) & (dist // dil <= window // dil)).astype(dist.dtype)
        out[d] = np.where(mult > 0, np.log2(np.maximum(mult, 1)), NEG)
    return out


OUT_TM = 1024
MIX_W = 3 * LANE


def _outproj_kernel(x_ref, oa_ref, ob_ref, oc_ref, ga_ref, gb_ref, gc_ref, wa_ref, wb_ref, wc_ref, o_ref):
    def normed(o_r, g_r, width):
        o = o_r[...]
        ms = jnp.sum(o * o, axis=-1, keepdims=True) * (1.0 / width)
        return (o * lax.rsqrt(ms + EPS) * g_r[...]).astype(BF16)

    ya = normed(oa_ref, ga_ref, H_FOX * HEAD_DIM)
    yb = normed(ob_ref, gb_ref, H_MOBA * HEAD_DIM)
    yc = normed(oc_ref, gc_ref, H_DIL * HEAD_DIM)
    o_ref[...] = x_ref[...] + _dot(ya, wa_ref[...]) + _dot(yb, wb_ref[...]) + _dot(yc, wc_ref[...])


def _outproj(x2, oa, ob, oc, ga, gb, gc, wa, wb, wc):
    T = x2.shape[0]
    row = lambda w: pl.BlockSpec((OUT_TM, w), lambda i: (i, 0))
    full = lambda a, b: pl.BlockSpec((a, b), lambda i: (0, 0))
    return pl.pallas_call(
        _outproj_kernel,
        grid=(T // OUT_TM,),
        in_specs=[row(D_MODEL), row(MIX_W), row(MIX_W), row(MIX_W),
                  full(1, MIX_W), full(1, MIX_W), full(1, MIX_W),
                  full(MIX_W, D_MODEL), full(MIX_W, D_MODEL), full(MIX_W, D_MODEL)],
        out_specs=row(D_MODEL),
        out_shape=jax.ShapeDtypeStruct((T, D_MODEL), F32),
        compiler_params=_cparams(("parallel",)),
        name="outproj",
    )(x2, oa, ob, oc, ga, gb, gc, wa, wb, wc)


FFN_TM = 2048
FFN_TF = 512
FFN_SUB = 512


def _swiglu_act(a, b):
    return a * (1.0 / (1.0 + jnp.exp(-a))) * b


def _ffn_kernel(x_ref, g_ref, w1_ref, w3_ref, w2_ref, o_ref, h_scr):
    f = pl.program_id(1)

    @pl.when(f == 0)
    def _():
        x = x_ref[...]
        h_scr[...] = _rms(x, g_ref[...]).astype(BF16)
        o_ref[...] = x

    for c in range(FFN_TM // FFN_SUB):
        rows = slice(c * FFN_SUB, (c + 1) * FFN_SUB)
        h = h_scr[rows, :]
        act = _swiglu_act(_dot(h, w1_ref[...]), _dot(h, w3_ref[...])).astype(BF16)
        o_ref[rows, :] = o_ref[rows, :] + _dot(act, w2_ref[...])


def _ffn(x2, g, w1, w3, w2):
    T = x2.shape[0]
    return pl.pallas_call(
        _ffn_kernel,
        grid=(T // FFN_TM, D_FF // FFN_TF),
        in_specs=[
            pl.BlockSpec((FFN_TM, D_MODEL), lambda i, f: (i, 0)),
            pl.BlockSpec((1, D_MODEL), lambda i, f: (0, 0)),
            pl.BlockSpec((D_MODEL, FFN_TF), lambda i, f: (0, f)),
            pl.BlockSpec((D_MODEL, FFN_TF), lambda i, f: (0, f)),
            pl.BlockSpec((FFN_TF, D_MODEL), lambda i, f: (f, 0)),
        ],
        out_specs=pl.BlockSpec((FFN_TM, D_MODEL), lambda i, f: (i, 0)),
        out_shape=jax.ShapeDtypeStruct((T, D_MODEL), F32),
        scratch_shapes=[pltpu.VMEM((FFN_TM, D_MODEL), BF16)],
        compiler_params=_cparams(("parallel", "arbitrary")),
        name="ffn_dense",
    )(x2, g, w1, w3, w2)


MOE_TT = 2048
MOE_CH = 256
MOE_UNIT = 64
MOE_TF = 512
MOE_KB = 256
MOE_NKB = MOE_TT // MOE_KB
MOE_X_SLOTS = 4
MOE_GATHER_WIDTHS = (2, 6)


def _router_kernel(x_ref, g_ref, wr_cat_ref, wr_hi_ref, tri_ref,
                   h_ref, posc_ref, gatec_ref, posr_ref, cnt_ref):
    h = _rms(x_ref[...], g_ref[...])
    h_hi = h.astype(BF16)
    h_lo = (h - h_hi.astype(F32)).astype(BF16)
    h_ref[...] = h_hi
    hh = _dot(h_hi, wr_cat_ref[...])
    logits = hh[:, :LANE] + hh[:, LANE:] + _dot(h_lo, wr_hi_ref[...])
    lane = lax.broadcasted_iota(jnp.int32, (1, LANE), 1).astype(F32)
    lg = jnp.where(lane < N_EXPERTS, logits, -jnp.inf)
    m1 = jnp.max(lg, axis=1, keepdims=True)
    i1 = jnp.min(jnp.where(lg == m1, lane, float(LANE)), axis=1, keepdims=True)
    lg2 = jnp.where(lane == i1, -jnp.inf, lg)
    m2 = jnp.max(lg2, axis=1, keepdims=True)
    i2 = jnp.min(jnp.where(lg2 == m2, lane, float(LANE)), axis=1, keepdims=True)
    e2 = jnp.exp(m2 - m1)
    g1 = 1.0 / (1.0 + e2)
    g2 = e2 * g1
    routed = (lane == i1) | (lane == i2)
    gatec_ref[...] = jnp.where(lane == i1, g1, jnp.where(lane == i2, g2, 0.0))
    routed_f = jnp.where(routed, 1.0, 0.0)
    blocks = _cumsum_rows(routed_f, tri_ref[...], exact_small_ints=True)
    for b, cnt in enumerate(blocks):
        sl = slice(b * LANE, (b + 1) * LANE)
        pos = jnp.where(routed_f[sl] > 0.5, cnt - 1.0, -1.0)
        posc_ref[sl, :] = pos
        posr_ref[0, :, sl] = pos.T[:8, :]
    per_kb = MOE_KB // LANE
    for kb in range(MOE_NKB):
        cnt_ref[0, kb:kb + 1, :] = blocks[(kb + 1) * per_kb - 1][LANE - 1:LANE, :]


def _router(x2, g, wr_cat, wr_hi, tri):
    T = x2.shape[0]
    nt = T // MOE_TT
    return pl.pallas_call(
        _router_kernel,
        grid=(nt,),
        in_specs=[
            pl.BlockSpec((MOE_TT, D_MODEL), lambda i: (i, 0)),
            pl.BlockSpec((1, D_MODEL), lambda i: (0, 0)),
            pl.BlockSpec((D_MODEL, 2 * LANE), lambda i: (0, 0)),
            pl.BlockSpec((D_MODEL, LANE), lambda i: (0, 0)),
            pl.BlockSpec((LANE, LANE), lambda i: (0, 0)),
        ],
        out_specs=[
            pl.BlockSpec((MOE_TT, D_MODEL), lambda i: (i, 0)),
            pl.BlockSpec((MOE_TT, LANE), lambda i: (i, 0)),
            pl.BlockSpec((MOE_TT, LANE), lambda i: (i, 0)),
            pl.BlockSpec((1, 8, MOE_TT), lambda i: (i, 0, 0)),
            pl.BlockSpec((1, MOE_NKB, LANE), lambda i: (i, 0, 0)),
        ],
        out_shape=[
            jax.ShapeDtypeStruct((T, D_MODEL), BF16),
            jax.ShapeDtypeStruct((T, LANE), F32),
            jax.ShapeDtypeStruct((T, LANE), F32),
            jax.ShapeDtypeStruct((nt, 8, MOE_TT), F32),
            jax.ShapeDtypeStruct((nt, MOE_NKB, LANE), F32),
        ],
        compiler_params=_cparams(("parallel",)),
        name="moe_router",
    )(x2, g, wr_cat, wr_hi, tri)


def _moe_kernel(pref_ref, h_ref, posc_ref, gatec_ref, posr_ref, w1_ref, w3_ref, w2_ref, x_hbm, gfin_ref, o_ref,
                xe_scr, ye_scr, pe_scr, ge_scr, x_buf, x_sem):
    t, e, f = pl.program_id(0), pl.program_id(1), pl.program_id(2)
    nf = pl.num_programs(2)
    base = (t * N_EXPERTS + e) * (MOE_NKB + 1)
    pref = [pref_ref[base + kb] for kb in range(MOE_NKB + 1)]
    n_rows = pref[MOE_NKB]
    lane = lax.broadcasted_iota(jnp.int32, (1, LANE), 1)
    last_step = (f == nf - 1) & (e == N_EXPERTS - 1)

    def x_copy(kb):
        src = x_hbm.at[pl.ds(pl.multiple_of(t * MOE_TT + kb * MOE_KB, MOE_KB), MOE_KB), :]
        return pltpu.make_async_copy(src, x_buf.at[kb % MOE_X_SLOTS], x_sem.at[kb % MOE_X_SLOTS])

    @pl.when(last_step)
    def _():
        for kb in range(MOE_X_SLOTS):
            x_copy(kb).start()

    @pl.when((e == 0) & (f == 0))
    def _():
        o_ref[...] = jnp.zeros((MOE_TT, D_MODEL), F32)

    @pl.when(f == 0)
    def _():
        is_e = lane == e
        pe_scr[...] = jnp.sum(jnp.where(is_e, posc_ref[...], 0.0), axis=1, keepdims=True)
        ge_scr[...] = jnp.sum(jnp.where(is_e, gatec_ref[...], 0.0), axis=1, keepdims=True)
        slot = lax.broadcasted_iota(jnp.int32, (MOE_CH, 1), 0).astype(F32)

        def gather(c, _):
            r0 = pl.multiple_of(c * MOE_CH, MOE_CH)
            rows = pl.ds(r0, MOE_CH)
            slots = slot + r0.astype(F32)
            ye_scr[rows, :] = jnp.zeros((MOE_CH, D_MODEL), F32)
            kb_lo = sum((pref[kb + 1] <= r0).astype(jnp.int32) for kb in range(MOE_NKB))
            kb_hi = MOE_NKB - 1 - sum((pref[kb] >= r0 + MOE_CH).astype(jnp.int32) for kb in range(MOE_NKB))
            n_kb = kb_hi - kb_lo + 1

            def one_dot(width):
                t0 = pl.multiple_of(jnp.minimum(kb_lo, MOE_NKB - width) * MOE_KB, MOE_KB)
                tok = pl.ds(t0, width * MOE_KB)
                onehot = jnp.where(posr_ref[0, pl.ds(e, 1), tok] == slots, 1.0, 0.0).astype(BF16)
                xe_scr[rows, :] = _dot(onehot, h_ref[tok, :]).astype(BF16)

            lo_w, mid_w = MOE_GATHER_WIDTHS
            pl.when(n_kb <= lo_w)(lambda: one_dot(lo_w))
            pl.when((n_kb > lo_w) & (n_kb <= mid_w))(lambda: one_dot(mid_w))
            pl.when(n_kb > mid_w)(lambda: one_dot(MOE_NKB))
            return 0

        lax.fori_loop(0, (n_rows + MOE_CH - 1) // MOE_CH, gather, 0)

    def expert_rows(r0, n):
        rows = pl.ds(r0, n)
        xc = xe_scr[rows, :]
        act = _swiglu_act(_dot(xc, w1_ref[0]), _dot(xc, w3_ref[0])).astype(BF16)
        ye_scr[rows, :] = ye_scr[rows, :] + _dot(act, w2_ref[0])

    n_units = (n_rows + MOE_UNIT - 1) // MOE_UNIT
    units_per_trip = 2 * MOE_CH // MOE_UNIT
    half = MOE_CH // 2

    def chunks(r0, sizes):
        for n in sizes:
            expert_rows(r0, n)
            r0 = r0 + n

    def two_chunks(i, _):
        chunks(pl.multiple_of(i * (2 * MOE_CH), 2 * MOE_CH), (MOE_CH, MOE_CH))
        return 0

    n_trips = n_units // units_per_trip
    left = n_units - n_trips * units_per_trip
    ride = (left >= 1) & (left <= 2) & (n_trips >= 1)
    n_plain = n_trips - ride.astype(jnp.int32)
    lax.fori_loop(0, n_plain, two_chunks, 0)
    tail0 = pl.multiple_of(n_plain * (2 * MOE_CH), 2 * MOE_CH)
    pl.when(ride & (left == 1))(lambda: chunks(tail0, (MOE_CH, MOE_CH + MOE_UNIT)))
    pl.when(ride & (left == 2))(lambda: chunks(tail0, (MOE_CH, MOE_CH + 2 * MOE_UNIT)))
    tail_halves = jnp.where(ride, 0, (left * MOE_UNIT + half - 1) // half)
    pl.when(tail_halves == 1)(lambda: chunks(tail0, (half,)))
    pl.when(tail_halves == 2)(lambda: chunks(tail0, (MOE_CH,)))
    pl.when(tail_halves == 3)(lambda: chunks(tail0, (MOE_CH, half)))
    pl.when(tail_halves == 4)(lambda: chunks(tail0, (MOE_CH, MOE_CH)))

    @pl.when(f == nf - 1)
    def _():
        for kb in range(MOE_NKB):
            lo, hi = pref[kb], pref[kb + 1]
            tok = slice(kb * MOE_KB, (kb + 1) * MOE_KB)
            w_lo = lo // MOE_CH
            w_hi = (hi - 1) // MOE_CH

            def combine(n_win, w0, tok=tok):
                r0 = pl.multiple_of(w0 * MOE_CH, MOE_CH)
                slots = (lax.broadcasted_iota(jnp.int32, (1, n_win * MOE_CH), 1) + r0).astype(F32)
                onehot_t = jnp.where(pe_scr[tok, :] == slots, 1.0, 0.0).astype(BF16)
                y = ye_scr[pl.ds(r0, n_win * MOE_CH), :].astype(BF16)
                o_ref[tok, :] = o_ref[tok, :] + ge_scr[tok, :] * _dot(onehot_t, y)

            pl.when((hi > lo) & (w_hi == w_lo))(functools.partial(combine, 1, w_lo))
            pl.when((hi > lo) & (w_hi > w_lo))(functools.partial(combine, 2, w_lo))

    @pl.when(last_step)
    def _():
        for kb in range(MOE_NKB):
            x_copy(kb).wait()
            tok = slice(kb * MOE_KB, (kb + 1) * MOE_KB)
            o_ref[tok, :] = _rms(x_buf[kb % MOE_X_SLOTS] + o_ref[tok, :], gfin_ref[...])
            if kb + MOE_X_SLOTS < MOE_NKB:
                x_copy(kb + MOE_X_SLOTS).start()


def _moe(counts, h, posc, gatec, posr, we1, we3, we2, x2, g_final):
    T = h.shape[0]
    nt = T // MOE_TT
    grid_spec = pltpu.PrefetchScalarGridSpec(
        num_scalar_prefetch=1,
        grid=(nt, N_EXPERTS, D_FF // MOE_TF),
        in_specs=[
            pl.BlockSpec((MOE_TT, D_MODEL), lambda t, e, f, c: (t, 0)),
            pl.BlockSpec((MOE_TT, LANE), lambda t, e, f, c: (t, 0)),
            pl.BlockSpec((MOE_TT, LANE), lambda t, e, f, c: (t, 0)),
            pl.BlockSpec((1, 8, MOE_TT), lambda t, e, f, c: (t, 0, 0)),
            pl.BlockSpec((1, D_MODEL, MOE_TF), lambda t, e, f, c: (e, 0, f)),
            pl.BlockSpec((1, D_MODEL, MOE_TF), lambda t, e, f, c: (e, 0, f)),
            pl.BlockSpec((1, MOE_TF, D_MODEL), lambda t, e, f, c: (e, f, 0)),
            pl.BlockSpec(memory_space=pl.ANY),
            pl.BlockSpec((1, D_MODEL), lambda t, e, f, c: (0, 0)),
        ],
        out_specs=pl.BlockSpec((MOE_TT, D_MODEL), lambda t, e, f, c: (t, 0)),
        scratch_shapes=[
            pltpu.VMEM((MOE_TT, D_MODEL), BF16),
            pltpu.VMEM((MOE_TT, D_MODEL), F32),
            pltpu.VMEM((MOE_TT, 1), F32),
            pltpu.VMEM((MOE_TT, 1), F32),
            pltpu.VMEM((MOE_X_SLOTS, MOE_KB, D_MODEL), F32),
            pltpu.SemaphoreType.DMA((MOE_X_SLOTS,)),
        ],
    )
    return pl.pallas_call(
        _moe_kernel,
        grid_spec=grid_spec,
        out_shape=jax.ShapeDtypeStruct((T, D_MODEL), F32),
        compiler_params=_cparams(("parallel", "arbitrary", "arbitrary")),
        name="moe_experts",
    )(counts, h, posc, gatec, posr, we1, we3, we2, x2, g_final)


def _rope_lane_tables():
    half = ROT_DIM // 2
    inv = ROPE_THETA ** (-jnp.arange(0, ROT_DIM, 2, dtype=F32) / ROT_DIM)
    ang = jnp.arange(SEQ, dtype=F32)[:, None] * inv[None, :]
    cos, sin = jnp.cos(ang), jnp.sin(ang)
    ones = jnp.ones((SEQ, HEAD_DIM - ROT_DIM), F32)
    zeros = jnp.zeros((SEQ, HEAD_DIM - ROT_DIM), F32)
    z8 = jnp.zeros((SEQ, half), F32)
    cos_h = jnp.concatenate([cos, cos, ones], axis=1)
    sp_h = jnp.concatenate([z8, sin, zeros], axis=1)
    sm_h = jnp.concatenate([-sin, z8, zeros], axis=1)
    two = lambda a: jnp.concatenate([a, a], axis=1)
    return two(cos_h), two(sp_h), two(sm_h)


def _pad_lanes(a, width=LANE):
    return jnp.pad(a, ((0, 0), (0, width - a.shape[1])))


def _mixer(x2, B, norm_mix, w_in, b_f, mix_gain, w_out, tables, tri, lm):
    w_qk = w_in[:, :QK_W].astype(BF16)
    w_vt = w_in[:, QK_W:QKV_W].T.astype(BF16)
    w_f = _pad_lanes(w_in[:, QKV_W:]).astype(BF16)
    qk, vt3, f_logit = _inproj(x2, norm_mix[None, :], w_qk, w_vt, w_f, *tables)
    cbc, crow = _forget_cumsum(f_logit.reshape(B, SEQ, LANE), _pad_lanes(b_f[None, :]), tri)
    qk3 = qk.reshape(B, SEQ, QK_W)
    oa = _attention("fox", 0, H_FOX, qk3, vt3, (cbc, crow))
    ob = _attention("moba", H_FOX, H_MOBA, qk3, vt3, ())
    oc = _attention("dil", H_FOX + H_MOBA, H_DIL, qk3, vt3, (lm,))
    a0, b0, c0 = 0, (H_FOX // 2) * PAIR, ((H_FOX + H_MOBA) // 2) * PAIR
    T = x2.shape[0]
    sl = lambda v, s: v[s:s + MIX_W]
    return _outproj(
        x2, oa.reshape(T, MIX_W), ob.reshape(T, MIX_W), oc.reshape(T, MIX_W),
        sl(mix_gain, a0)[None, :], sl(mix_gain, b0)[None, :], sl(mix_gain, c0)[None, :],
        sl(w_out, a0).astype(BF16), sl(w_out, b0).astype(BF16), sl(w_out, c0).astype(BF16))


def kernel(x, l0_norm_mix, l0_w_in, l0_b_f, l0_mix_gain, l0_w_out, l0_norm_ffn, l0_w1, l0_w3, l0_w2, l1_norm_mix, l1_w_in, l1_b_f, l1_mix_gain, l1_w_out, l1_norm_ffn, l1_w_router, l1_we1, l1_we3, l1_we2, final_norm):
    B, S, D = x.shape
    assert (S, D) == (SEQ, D_MODEL)
    T = B * S
    tables = _rope_lane_tables()
    tri = jnp.asarray(np.tril(np.ones((LANE, LANE), np.float32)), BF16)
    lm = jnp.asarray(_dilated_log2_multiplicity_t())

    x2 = x.reshape(T, D)
    x2 = _mixer(x2, B, l0_norm_mix, l0_w_in, l0_b_f, l0_mix_gain, l0_w_out, tables, tri, lm)
    x2 = _ffn(x2, l0_norm_ffn[None, :], l0_w1.astype(BF16), l0_w3.astype(BF16), l0_w2.astype(BF16))
    x2 = _mixer(x2, B, l1_norm_mix, l1_w_in, l1_b_f, l1_mix_gain, l1_w_out, tables, tri, lm)

    wr = _pad_lanes(l1_w_router)
    wr_hi = wr.astype(BF16)
    wr_lo = (wr - wr_hi.astype(F32)).astype(BF16)
    wr_cat = jnp.concatenate([wr_hi, wr_lo], axis=1)
    h, posc, gatec, posr, cnt = _router(x2, l1_norm_ffn[None, :], wr_cat, wr_hi, tri)
    ends = cnt[:, :, :N_EXPERTS].astype(jnp.int32).transpose(0, 2, 1)
    pref = jnp.pad(ends, ((0, 0), (0, 0), (1, 0))).reshape(-1)
    out = _moe(pref, h, posc, gatec, posr,
               l1_we1.astype(BF16), l1_we3.astype(BF16), l1_we2.astype(BF16), x2, final_norm[None, :])
    return out.reshape(B, S, D)
```
